```python
import jax
import jax.numpy as jnp
from jax import lax
import numpy as np

D_MODEL = 1024
BATCH = 32
SEQ = 2048
DEPTH = 1
DEC_BATCH = 128
DEC_SEQ = 8
PAST_LEN = 8192
PAGE_SIZE = 128

N_HEADS = 8
HEAD_DIM = D_MODEL // N_HEADS
N_KV_HEADS = 4
KV_GROUP = N_HEADS // N_KV_HEADS
MOBA_BLOCK = 256
MOBA_TOPK = 3
Q_CHUNK = 8
ATTN_SCALE = HEAD_DIM ** -0.5
POOL_WINDOWS = (2, 4, 8, 16)
N_POOL_GROUPS = len(POOL_WINDOWS)
POOL_WIDTH = D_MODEL
POOL_GROUP_DIM = POOL_WIDTH // N_POOL_GROUPS
POOL_HIST = max(POOL_WINDOWS) - 1
D_FF = -(-8 * D_MODEL // (3 * 256)) * 256
RMS_EPS = 1e-6
Q_COLS = N_HEADS * HEAD_DIM
KV_COLS = N_KV_HEADS * HEAD_DIM
IN_COLS = Q_COLS + 2 * KV_COLS + POOL_WIDTH + 2 * D_MODEL
SPLITS = (Q_COLS, Q_COLS + KV_COLS, Q_COLS + 2 * KV_COLS,
          Q_COLS + 2 * KV_COLS + POOL_WIDTH, Q_COLS + 2 * KV_COLS + POOL_WIDTH + D_MODEL)

kernel_name = "moba_pool_gated_hybrid_step"


def rmsnorm(x, g):
    xf = x.astype(jnp.float32)
    y = xf * lax.rsqrt(jnp.mean(xf * xf, axis=-1, keepdims=True) + RMS_EPS)
    return (y * g.astype(jnp.float32)).astype(x.dtype)


def alibi_slopes():
    return 2.0 ** (-8.0 * jnp.arange(1, N_HEADS + 1, dtype=jnp.float32) / N_HEADS)


def project(x, ln1_g, w_in, q_norm_g, k_norm_g):
    B, T, _ = x.shape
    z = rmsnorm(x, ln1_g) @ w_in
    q, k, v, u, ga, gb = jnp.split(z, SPLITS, axis=-1)
    q = rmsnorm(q.reshape(B, T, N_HEADS, HEAD_DIM), q_norm_g).transpose(0, 2, 1, 3)
    k = rmsnorm(k.reshape(B, T, N_KV_HEADS, HEAD_DIM), k_norm_g).transpose(0, 2, 1, 3)
    v = v.reshape(B, T, N_KV_HEADS, HEAD_DIM).transpose(0, 2, 1, 3)
    return q, k, v, u, ga, gb


def moba_prompt(q, k, v):
    f32 = jnp.float32
    B, _, S, _ = q.shape
    nb = -(-S // MOBA_BLOCK)
    sp = nb * MOBA_BLOCK
    pad = ((0, 0), (0, 0), (0, sp - S), (0, 0))
    q, k, v = jnp.pad(q, pad), jnp.pad(k, pad), jnp.pad(v, pad)
    kb = k.reshape(B, N_KV_HEADS, nb, MOBA_BLOCK, HEAD_DIM)
    vb = v.reshape(B, N_KV_HEADS, nb, MOBA_BLOCK, HEAD_DIM)
    topk = min(MOBA_TOPK, nb - 1)
    slopes = alibi_slopes()
    kv_head = jnp.arange(N_HEADS) // KV_GROUP
    b_ix = jnp.arange(B)[:, None, None, None]
    nc = sp // Q_CHUNK
    to_chunks = lambda a: jnp.moveaxis(a.reshape(B, N_HEADS, nc, Q_CHUNK, *a.shape[3:]), 2, 0)
    xs = (jnp.arange(nc), to_chunks(q))
    if topk > 0:
        kmean = jnp.mean(kb.astype(f32), axis=3)
        qg = q.astype(f32).reshape(B, N_KV_HEADS, KV_GROUP, sp, HEAD_DIM)
        gate = jnp.einsum('bkgsd,bknd->bkgsn', qg, kmean).reshape(B, N_HEADS, sp, nb)
        fully_past = jnp.arange(nb)[None, :] < (jnp.arange(sp) // MOBA_BLOCK)[:, None]
        gate = jnp.where(fully_past, gate, -jnp.inf)
        gval, gidx = lax.top_k(gate, topk)
        xs = xs + (to_chunks(gidx), to_chunks(jnp.isfinite(gval)))

    def step(args):
        c, q_c = args[0], args[1]
        t = c * Q_CHUNK + jnp.arange(Q_CHUNK)
        blk = (c * Q_CHUNK) // MOBA_BLOCK
        k_own = lax.dynamic_index_in_dim(kb, blk, axis=2, keepdims=False)
        v_own = lax.dynamic_index_in_dim(vb, blk, axis=2, keepdims=False)
        s_own = blk * MOBA_BLOCK + jnp.arange(MOBA_BLOCK)
        qg_c = q_c.reshape(B, N_KV_HEADS, KV_GROUP, Q_CHUNK, HEAD_DIM)
        lo = jnp.einsum('bkgqd,bkrd->bkgqr', qg_c, k_own, preferred_element_type=f32)
        lo = lo.reshape(B, N_HEADS, Q_CHUNK, MOBA_BLOCK) * ATTN_SCALE
        dist = t[:, None] - s_own[None, :]
        lo = jnp.where(dist >= 0, lo - slopes[:, None, None] * dist.astype(f32), -jnp.inf)
        logits = [lo]
        if topk > 0:
            idx_c, ok_c = args[2], args[3]
            k_sel = kb[b_ix, kv_head[None, :, None, None], idx_c]
            v_sel = vb[b_ix, kv_head[None, :, None, None], idx_c]
            ls = jnp.einsum('bhqd,bhqjrd->bhqjr', q_c, k_sel, preferred_element_type=f32) * ATTN_SCALE
            s_sel = idx_c[..., None] * MOBA_BLOCK + jnp.arange(MOBA_BLOCK)
            ls = ls - slopes[None, :, None, None, None] * (t[None, None, :, None, None] - s_sel).astype(f32)
            ls = jnp.where(ok_c[..., None], ls, -jnp.inf).reshape(B, N_HEADS, Q_CHUNK, topk * MOBA_BLOCK)
            logits = [ls, lo]
        p = jax.nn.softmax(jnp.concatenate(logits, axis=-1), axis=-1)
        p_own = p[..., -MOBA_BLOCK:].reshape(B, N_KV_HEADS, KV_GROUP, Q_CHUNK, MOBA_BLOCK)
        out = jnp.einsum('bkgqr,bkrd->bkgqd', p_own, v_own.astype(f32)).reshape(B, N_HEADS, Q_CHUNK, HEAD_DIM)
        if topk > 0:
            vs = v_sel.reshape(B, N_HEADS, Q_CHUNK, topk * MOBA_BLOCK, HEAD_DIM).astype(f32)
            out = out + jnp.einsum('bhqn,bhqnd->bhqd', p[..., :topk * MOBA_BLOCK], vs)
        return out.astype(q_c.dtype)

    o = lax.map(step, xs)
    o = o.transpose(1, 0, 3, 2, 4).reshape(B, sp, N_HEADS * HEAD_DIM)
    return o[:, :S]


def moba_sample(q, k_new, v_new, cache_k, cache_v, page_table, layer):
    f32 = jnp.float32
    DB, _, DS, _ = q.shape
    n_pages = page_table.shape[1]
    past = n_pages * PAGE_SIZE
    ppb = MOBA_BLOCK // PAGE_SIZE
    b0 = past // MOBA_BLOCK
    n_full_pages = b0 * ppb
    n_own = n_pages - n_full_pages
    topk = min(MOBA_TOPK, b0)
    slopes = alibi_slopes()
    kv_head = jnp.arange(N_HEADS) // KV_GROUP
    t = past + jnp.arange(DS)
    own_pages = page_table[:, n_full_pages:]

    def own_rows(cache):
        r = cache[layer, own_pages]
        return r.transpose(0, 2, 1, 3, 4).reshape(DB, N_KV_HEADS, n_own * PAGE_SIZE, HEAD_DIM)

    k_own = jnp.concatenate([own_rows(cache_k), k_new], axis=2)
    v_own = jnp.concatenate([own_rows(cache_v), v_new], axis=2)
    s_own = jnp.concatenate([n_full_pages * PAGE_SIZE + jnp.arange(n_own * PAGE_SIZE), t])
    L = k_own.shape[2]
    xs = (jnp.moveaxis(q, 2, 0), t)
    if topk > 0:
        page_sum = lax.map(lambda pt: jnp.sum(cache_k[layer, pt[:n_full_pages]].astype(f32), axis=2), page_table)
        kmean = page_sum.reshape(DB, b0, ppb, N_KV_HEADS, HEAD_DIM).sum(axis=2) / MOBA_BLOCK
        qg = q.astype(f32).reshape(DB, N_KV_HEADS, KV_GROUP, DS, HEAD_DIM)
        gate = jnp.einsum('bkgqd,bnkd->bkgqn', qg, kmean).reshape(DB, N_HEADS, DS, b0)
        _, idx = lax.top_k(gate, topk)
        logical = idx[..., None] * ppb + jnp.arange(ppb)
        phys = page_table[jnp.arange(DB)[:, None, None, None, None], logical]
        xs = xs + (jnp.moveaxis(idx, 2, 0), jnp.moveaxis(phys, 2, 0))

    def step(args):
        q_i, t_i = args[0], args[1]
        qg_i = q_i.reshape(DB, N_KV_HEADS, KV_GROUP, HEAD_DIM)
        lo = jnp.einsum('bkgd,bkrd->bkgr', qg_i, k_own, preferred_element_type=f32).reshape(DB, N_HEADS, L) * ATTN_SCALE
        dist = t_i - s_own
        lo = jnp.where(dist >= 0, lo - slopes[:, None] * dist.astype(f32)[None, :], -jnp.inf)
        logits = [lo]
        if topk > 0:
            idx_i, phys_i = args[2], args[3]
            k_sel = cache_k[layer, phys_i, kv_head[None, :, None, None]]
            v_sel = cache_v[layer, phys_i, kv_head[None, :, None, None]]
            ls = jnp.einsum('bhd,bhjprd->bhjpr', q_i, k_sel, preferred_element_type=f32)
            ls = ls.reshape(DB, N_HEADS, topk * MOBA_BLOCK) * ATTN_SCALE
            s_sel = (idx_i[..., None] * MOBA_BLOCK + jnp.arange(MOBA_BLOCK)).reshape(DB, N_HEADS, topk * MOBA_BLOCK)
            ls = ls - slopes[None, :, None] * (t_i - s_sel).astype(f32)
            logits = [ls, lo]
        p = jax.nn.softmax(jnp.concatenate(logits, axis=-1), axis=-1)
        p_own = p[..., -L:].reshape(DB, N_KV_HEADS, KV_GROUP, L)
        out = jnp.einsum('bkgr,bkrd->bkgd', p_own, v_own.astype(f32)).reshape(DB, N_HEADS, HEAD_DIM)
        if topk > 0:
            vs = v_sel.reshape(DB, N_HEADS, topk * MOBA_BLOCK, HEAD_DIM).astype(f32)
            out = out + jnp.einsum('bhn,bhnd->bhd', p[..., :topk * MOBA_BLOCK], vs)
        return out.astype(q_i.dtype)

    o = lax.map(step, xs)
    return o.transpose(1, 0, 2, 3).reshape(DB, DS, N_HEADS * HEAD_DIM)


def pool_mix(u_ext, pos, w_pool, pool_scale):
    B = u_ext.shape[0]
    T = pos.shape[0]
    uf = u_ext.astype(jnp.float32)
    cs = jnp.pad(jnp.cumsum(uf, axis=1), ((0, 0), (1, 0), (0, 0)))
    cur = uf[:, POOL_HIST:]
    diffs = []
    for g, w in enumerate(POOL_WINDOWS):
        sl = slice(g * POOL_GROUP_DIM, (g + 1) * POOL_GROUP_DIM)
        win = cs[:, POOL_HIST + 1:POOL_HIST + 1 + T, sl] - cs[:, POOL_HIST + 1 - w:POOL_HIST + 1 - w + T, sl]
        cnt = jnp.minimum(w, pos + 1).astype(jnp.float32)[None, :, None]
        diffs.append(win / cnt - cur[..., sl])
    d = jnp.stack(diffs, axis=2)
    y = jnp.einsum('btgc,gce->btge', d, w_pool.astype(jnp.float32)).reshape(B, T, POOL_WIDTH)
    return (y * pool_scale.astype(jnp.float32)).astype(u_ext.dtype)


def merge_and_ffn(x, attn, pool, ga, gb, w_out, ln2_g, w_gate, w_up, w_down):
    f32 = jnp.float32
    h = jax.nn.sigmoid(ga.astype(f32)) * attn.astype(f32) + jax.nn.sigmoid(gb.astype(f32)) * pool.astype(f32)
    x = x + h.astype(x.dtype) @ w_out
    hn = rmsnorm(x, ln2_g)
    return x + (jax.nn.silu(hn @ w_gate) * (hn @ w_up)) @ w_down


def setup_inputs(seed: int = 0) -> dict:
    key = jax.random.key(seed)
    ks = jax.random.split(key, 18)
    f32 = jnp.float32
    n_pages = PAST_LEN // PAGE_SIZE
    n_pool = -(-DEC_BATCH * n_pages * 5 // 4)

    def nrm(k, shape, scale):
        return jax.random.normal(k, shape, f32) * scale

    page_table = jax.random.permutation(ks[5], n_pool)[:DEC_BATCH * n_pages]
    page_table = page_table.reshape(DEC_BATCH, n_pages).astype(jnp.int32)
    return {
        "x_prompt": nrm(ks[0], (BATCH, SEQ, D_MODEL), 1.0),
        "x_sample": nrm(ks[1], (DEC_BATCH, DEC_SEQ, D_MODEL), 1.0),
        "cache_k": nrm(ks[2], (DEPTH, n_pool, N_KV_HEADS, PAGE_SIZE, HEAD_DIM), 1.0),
        "cache_v": nrm(ks[3], (DEPTH, n_pool, N_KV_HEADS, PAGE_SIZE, HEAD_DIM), 1.0),
        "state_pool": nrm(ks[4], (DEPTH, DEC_BATCH, POOL_HIST, POOL_WIDTH), 1.0),
        "page_table": page_table,
        "ln1_g": 1.0 + nrm(ks[6], (DEPTH, D_MODEL), 0.02),
        "w_in": nrm(ks[7], (DEPTH, D_MODEL, IN_COLS), D_MODEL ** -0.5),
        "q_norm_g": 1.0 + nrm(ks[8], (DEPTH, HEAD_DIM), 0.02),
        "k_norm_g": 1.0 + nrm(ks[9], (DEPTH, HEAD_DIM), 0.02),
        "w_pool": nrm(ks[10], (DEPTH, N_POOL_GROUPS, POOL_GROUP_DIM, POOL_GROUP_DIM), POOL_GROUP_DIM ** -0.5),
        "pool_scale": 1.0 + nrm(ks[11], (DEPTH, POOL_WIDTH), 0.02),
        "w_out": nrm(ks[12], (DEPTH, D_MODEL, D_MODEL), D_MODEL ** -0.5),
        "ln2_g": 1.0 + nrm(ks[13], (DEPTH, D_MODEL), 0.02),
        "w_gate": nrm(ks[14], (DEPTH, D_MODEL, D_FF), D_MODEL ** -0.5),
        "w_up": nrm(ks[15], (DEPTH, D_MODEL, D_FF), D_MODEL ** -0.5),
        "w_down": nrm(ks[16], (DEPTH, D_FF, D_MODEL), D_FF ** -0.5),
    }


def reference(x_prompt, x_sample, cache_k, cache_v, state_pool, page_table, ln1_g, w_in, q_norm_g, k_norm_g,
              w_pool, pool_scale, w_out, ln2_g, w_gate, w_up, w_down):
    past_len = page_table.shape[1] * PAGE_SIZE
    x_p, x_s = x_prompt, x_sample
    S = x_p.shape[1]
    T = x_s.shape[1]
    kp_l, vp_l, pp_l, ks_l, vs_l, ps_l = [], [], [], [], [], []
    for l in range(DEPTH):
        q, k, v, u, ga, gb = project(x_p, ln1_g[l], w_in[l], q_norm_g[l], k_norm_g[l])
        attn = moba_prompt(q, k, v)
        u_ext = jnp.pad(u, ((0, 0), (POOL_HIST, 0), (0, 0)))
        pool = pool_mix(u_ext, jnp.arange(S), w_pool[l], pool_scale[l])
        x_p = merge_and_ffn(x_p, attn, pool, ga, gb, w_out[l], ln2_g[l], w_gate[l], w_up[l], w_down[l])
        kp_l.append(k)
        vp_l.append(v)
        pp_l.append(u_ext[:, -POOL_HIST:])
        q, k, v, u, ga, gb = project(x_s, ln1_g[l], w_in[l], q_norm_g[l], k_norm_g[l])
        attn = moba_sample(q, k, v, cache_k, cache_v, page_table, l)
        u_ext = jnp.concatenate([state_pool[l].astype(u.dtype), u], axis=1)
        pool = pool_mix(u_ext, past_len + jnp.arange(T), w_pool[l], pool_scale[l])
        x_s = merge_and_ffn(x_s, attn, pool, ga, gb, w_out[l], ln2_g[l], w_gate[l], w_up[l], w_down[l])
        ks_l.append(k)
        vs_l.append(v)
        ps_l.append(u_ext[:, -POOL_HIST:])
    return (x_p, x_s, jnp.stack(kp_l), jnp.stack(vp_l), jnp.stack(pp_l), jnp.stack(ks_l), jnp.stack(vs_l), jnp.stack(ps_l))
```

```python
import functools

import jax
import jax.numpy as jnp
from jax import lax
from jax.experimental import pallas as pl
from jax.experimental.pallas import tpu as pltpu

D_MODEL = 1024
N_HEADS = 8
HEAD_DIM = D_MODEL // N_HEADS
N_KV_HEADS = 4
KV_GROUP = N_HEADS // N_KV_HEADS
MOBA_BLOCK = 256
MOBA_TOPK = 3
PAGE_SIZE = 128
ATTN_SCALE = HEAD_DIM ** -0.5
POOL_WINDOWS = (2, 4, 8, 16)
N_POOL_GROUPS = len(POOL_WINDOWS)
POOL_WIDTH = D_MODEL
POOL_GROUP_DIM = POOL_WIDTH // N_POOL_GROUPS
POOL_HIST = max(POOL_WINDOWS) - 1
RMS_EPS = 1e-6
Q_COLS = N_HEADS * HEAD_DIM
KV_COLS = N_KV_HEADS * HEAD_DIM
COL_Q = 0
COL_K = Q_COLS
COL_V = COL_K + KV_COLS
COL_U = COL_V + KV_COLS
COL_GA = COL_U + POOL_WIDTH
COL_GB = COL_GA + D_MODEL
IN_COLS = COL_GB + D_MODEL

SUBLANES = 8
LANES = 128
MXU_DIM = 256
VMEM_LIMIT_BYTES = 56 * 1024 * 1024

HIST_ROWS = 16
assert HIST_ROWS >= POOL_HIST and HIST_ROWS % SUBLANES == 0
MASK_NEG = -(2.0 ** 30)

TOKEN_TILE = 256
PAGES_PER_STEP = 16
BLOCKS_PER_STEP = PAGES_PER_STEP * PAGE_SIZE // MOBA_BLOCK
AUX_COLS = 64

_f32 = jnp.float32
_bf16 = jnp.bfloat16


def _nt_dot(a, b, precision=None):
    return lax.dot_general(a, b, (((1,), (1,)), ((), ())), precision=precision,
                           preferred_element_type=_f32)


def _dot(a, b):
    return jnp.dot(a, b, preferred_element_type=_f32)


def _rms(x, g):
    return x * lax.rsqrt(jnp.mean(x * x, axis=-1, keepdims=True) + RMS_EPS) * g


def _sigmoid(x):
    return 1.0 / (1.0 + jnp.exp(-x))


def _resident(shape):
    nd = len(shape)
    return pl.BlockSpec(shape, lambda *_: (0,) * nd, pipeline_mode=pl.Buffered(1))


def _proj_kernel(x_ref, ln1_ref, w_ref, qg_ref, kg_ref,
                 q_ref, k_ref, v_ref, u_ref, ga_ref, gb_ref, *rest, tm, windows):
    xn = _rms(x_ref[0], ln1_ref[...]).astype(_bf16)

    zq = _dot(xn, w_ref[:, COL_Q:COL_K])
    qg = qg_ref[...]
    for h in range(N_HEADS):
        sl = slice(h * HEAD_DIM, (h + 1) * HEAD_DIM)
        q_ref[0, :, sl] = _rms(zq[:, sl], qg)

    zk = _dot(xn, w_ref[:, COL_K:COL_V])
    zv = _dot(xn, w_ref[:, COL_V:COL_U])
    kg = kg_ref[...]
    for h in range(N_KV_HEADS):
        sl = slice(h * HEAD_DIM, (h + 1) * HEAD_DIM)
        k_ref[0, h] = _rms(zk[:, sl], kg)
        v_ref[0, h] = zv[:, sl]

    ga_ref[0] = _dot(xn, w_ref[:, COL_GA:COL_GB])
    gb_ref[0] = _dot(xn, w_ref[:, COL_GB:IN_COLS])

    zu = _dot(xn, w_ref[:, COL_U:COL_GA])
    if not windows:
        u_ref[0] = zu
        return

    hist_ref, uext_ref = rest
    si = pl.program_id(1)

    @pl.when(si == 0)
    def _():
        uext_ref[0:HIST_ROWS, :] = jnp.zeros((HIST_ROWS, POOL_WIDTH), _f32)

    uext_ref[HIST_ROWS:HIST_ROWS + tm, :] = zu
    pos = si * tm + lax.broadcasted_iota(jnp.int32, (tm, 1), 0)
    for g, w in enumerate(POOL_WINDOWS):
        sl = slice(g * POOL_GROUP_DIM, (g + 1) * POOL_GROUP_DIM)
        win = zu[:, sl]
        for dd in range(1, w):
            win = win + uext_ref[HIST_ROWS - dd:HIST_ROWS - dd + tm, sl]
        cnt = jnp.minimum(w, pos + 1).astype(_f32)
        u_ref[0, :, sl] = win / cnt - zu[:, sl]
    hist_ref[0] = zu[tm - HIST_ROWS:tm, :]
    uext_ref[0:HIST_ROWS, :] = zu[tm - HIST_ROWS:tm, :]


def _proj(x, ln1_g, w_in_bf16, q_norm_g, k_norm_g, *, windows):
    B, S, D = x.shape
    tm = min(TOKEN_TILE, S)
    assert S % tm == 0 and tm % HIST_ROWS == 0
    tok = lambda n: pl.BlockSpec((1, tm, n), lambda b, s: (b, s, 0))
    kv = pl.BlockSpec((1, N_KV_HEADS, tm, HEAD_DIM), lambda b, s: (b, 0, s, 0))
    out_shape = [
        jax.ShapeDtypeStruct((B, S, Q_COLS), _f32),
        jax.ShapeDtypeStruct((B, N_KV_HEADS, S, HEAD_DIM), _f32),
        jax.ShapeDtypeStruct((B, N_KV_HEADS, S, HEAD_DIM), _f32),
        jax.ShapeDtypeStruct((B, S, POOL_WIDTH), _f32),
        jax.ShapeDtypeStruct((B, S, D_MODEL), _f32),
        jax.ShapeDtypeStruct((B, S, D_MODEL), _f32),
    ]
    out_specs = [tok(Q_COLS), kv, kv, tok(POOL_WIDTH), tok(D_MODEL), tok(D_MODEL)]
    scratch = []
    if windows:
        out_shape.append(jax.ShapeDtypeStruct((B, HIST_ROWS, POOL_WIDTH), _f32))
        out_specs.append(pl.BlockSpec((1, HIST_ROWS, POOL_WIDTH), lambda b, s: (b, 0, 0)))
        scratch.append(pltpu.VMEM((HIST_ROWS + tm, POOL_WIDTH), _f32))
    return pl.pallas_call(
        functools.partial(_proj_kernel, tm=tm, windows=windows),
        out_shape=out_shape,
        grid=(B, S // tm),
        in_specs=[tok(D), _resident((1, D)), _resident((D, IN_COLS)),
                  _resident((1, HEAD_DIM)), _resident((1, HEAD_DIM))],
        out_specs=out_specs,
        scratch_shapes=scratch,
        compiler_params=pltpu.CompilerParams(
            dimension_semantics=("arbitrary", "arbitrary"), vmem_limit_bytes=VMEM_LIMIT_BYTES),
        name="proj_windows" if windows else "proj",
    )(x, ln1_g.reshape(1, D), w_in_bf16, q_norm_g.reshape(1, HEAD_DIM), k_norm_g.reshape(1, HEAD_DIM))


def _topk_select(g, valid, row_idx, n_rows, axis):
    gm = jnp.where(valid, g, -jnp.inf)
    cnt = jnp.zeros(g.shape, jnp.int32)
    for jp in range(n_rows):
        gj = lax.slice_in_dim(gm, jp, jp + 1, axis=axis)
        beats = (gj > gm) | ((gj == gm) & (jp < row_idx))
        cnt = cnt + jnp.where(beats, 1, 0)
    return valid & (cnt < MOBA_TOPK)


def _attn_prompt_kernel(q_ref, k_ref, v_ref, slope_ref, o_ref, kaug_ref, vb_ref, kmean_ref, *, nb, nbp):
    i = pl.program_id(2)
    S = nb * MOBA_BLOCK
    QR = KV_GROUP * MOBA_BLOCK

    @pl.when(i == 0)
    def _():
        k = k_ref[0, 0]
        kaug_ref[:, 0:HEAD_DIM] = k.astype(_bf16)
        row = lax.broadcasted_iota(jnp.int32, (S, LANES), 0)
        lane = lax.broadcasted_iota(jnp.int32, (S, LANES), 1)
        onehot = jnp.where(lane == row // MOBA_BLOCK, 1.0, 0.0)
        rpos = jnp.where(lane == nbp, (row % MOBA_BLOCK).astype(_f32), 0.0)
        kaug_ref[:, HEAD_DIM:HEAD_DIM + LANES] = (onehot + rpos).astype(_bf16)
        vb_ref[...] = v_ref[0, 0].astype(_bf16)
        kmean_ref[...] = jnp.zeros((nbp, HEAD_DIM), _f32)
        kmean_ref[0:nb, :] = jnp.mean(k.reshape(nb, MOBA_BLOCK, HEAD_DIM), axis=1)

    q2 = q_ref[0]
    qs = jnp.concatenate([q2[:, h * HEAD_DIM:(h + 1) * HEAD_DIM] for h in range(KV_GROUP)], axis=0)

    gate_t = _nt_dot(kmean_ref[...], qs, precision=lax.Precision.HIGHEST)
    jidx = lax.broadcasted_iota(jnp.int32, (nbp, QR), 0)
    sel = _topk_select(gate_t, jidx < i, jidx, nb, axis=0)
    slope_rows = slope_ref[0]
    slope = slope_rows[0:1, :]
    bias_t = jnp.where(sel, slope * (MOBA_BLOCK * (jidx - i)).astype(_f32), MASK_NEG)
    bias_t = jnp.where(jidx == i, 0.0, bias_t)
    aux_t = jnp.concatenate(
        [bias_t, slope_rows, jnp.zeros((LANES - nbp - SUBLANES, QR), _f32)], axis=0)
    q_aug = jnp.concatenate([(qs * ATTN_SCALE).astype(_bf16), aux_t.T.astype(_bf16)], axis=1)

    def scores(j):
        off = pl.multiple_of(j * MOBA_BLOCK, MOBA_BLOCK)
        return _nt_dot(q_aug, kaug_ref[pl.ds(off, MOBA_BLOCK), :]), off

    s, off = scores(i)
    t_rel = lax.broadcasted_iota(jnp.int32, (QR, MOBA_BLOCK), 0) % MOBA_BLOCK
    r_rel = lax.broadcasted_iota(jnp.int32, (QR, MOBA_BLOCK), 1)
    s = jnp.where(r_rel <= t_rel, s, MASK_NEG)
    m = jnp.max(s, axis=-1, keepdims=True)
    p = jnp.exp(s - m)
    l = jnp.sum(p, axis=-1, keepdims=True)
    acc = _dot(p.astype(_bf16), vb_ref[pl.ds(off, MOBA_BLOCK), :])

    def body(j, carry):
        m, l, acc = carry
        s, off = scores(j)
        m_new = jnp.maximum(m, jnp.max(s, axis=-1, keepdims=True))
        a = jnp.exp(m - m_new)
        p = jnp.exp(s - m_new)
        l = a * l + jnp.sum(p, axis=-1, keepdims=True)
        acc = a * acc + _dot(p.astype(_bf16), vb_ref[pl.ds(off, MOBA_BLOCK), :])
        return m_new, l, acc

    m, l, acc = lax.fori_loop(0, i, body, (m, l, acc))
    o = acc / l
    o_ref[0] = jnp.concatenate([o[h * MOBA_BLOCK:(h + 1) * MOBA_BLOCK] for h in range(KV_GROUP)], axis=1)


def _slope_table(rows_per_head):
    slopes = 2.0 ** (-8.0 * jnp.arange(1, N_HEADS + 1, dtype=_f32) / N_HEADS)
    per_row = jnp.repeat(slopes.reshape(N_KV_HEADS, KV_GROUP), rows_per_head, axis=1)
    tab = jnp.zeros((N_KV_HEADS, SUBLANES, KV_GROUP * rows_per_head), _f32)
    return tab.at[:, 0, :].set(per_row)


def _attn_prompt(q, k, v):
    B, S, _ = q.shape
    assert S % MOBA_BLOCK == 0
    nb = S // MOBA_BLOCK
    nbp = -(-nb // SUBLANES) * SUBLANES
    assert nbp + SUBLANES <= LANES
    gw = KV_GROUP * HEAD_DIM
    QR = KV_GROUP * MOBA_BLOCK
    qo_spec = pl.BlockSpec((1, MOBA_BLOCK, gw), lambda b, h, i: (b, i, h))
    kv_spec = pl.BlockSpec((1, 1, S, HEAD_DIM), lambda b, h, i: (b, h, 0, 0))
    return pl.pallas_call(
        functools.partial(_attn_prompt_kernel, nb=nb, nbp=nbp),
        out_shape=jax.ShapeDtypeStruct((B, S, Q_COLS), _f32),
        grid=(B, N_KV_HEADS, nb),
        in_specs=[qo_spec, kv_spec, kv_spec,
                  pl.BlockSpec((1, SUBLANES, QR), lambda b, h, i: (h, 0, 0))],
        out_specs=qo_spec,
        scratch_shapes=[pltpu.VMEM((S, HEAD_DIM + LANES), _bf16),
                        pltpu.VMEM((S, HEAD_DIM), _bf16),
                        pltpu.VMEM((nbp, HEAD_DIM), _f32)],
        compiler_params=pltpu.CompilerParams(
            dimension_semantics=("arbitrary", "arbitrary", "arbitrary"),
            vmem_limit_bytes=VMEM_LIMIT_BYTES),
        name="attn_prompt",
    )(q, k, v, _slope_table(MOBA_BLOCK))


def _attn_sample_kernel(pt_ref, q_ref, kn_ref, vn_ref, slope_ref, expand_ref, *rest, nc, ds, nblk):
    del pt_ref
    P = PAGES_PER_STEP
    kc_refs, vc_refs = rest[:P], rest[P:2 * P]
    o_ref = rest[2 * P]
    s_ref, p_ref, ksul_ref, acc_ref, l_ref = rest[2 * P + 1:]
    c = pl.program_id(1)
    QR = KV_GROUP * ds
    CW = P * PAGE_SIZE

    def stacked_q(h):
        q = q_ref[0]
        return jnp.concatenate(
            [q[:, (KV_GROUP * h + g) * HEAD_DIM:(KV_GROUP * h + g + 1) * HEAD_DIM] for g in range(KV_GROUP)],
            axis=0)

    @pl.when(c == 0)
    def _():
        ksul_ref[...] = jnp.zeros(ksul_ref.shape, _f32)

    @pl.when(c < nc)
    def _():
        for h in range(N_KV_HEADS):
            qs = (stacked_q(h) * ATTN_SCALE).astype(_bf16)
            sums = []
            for pg in range(P):
                kp = kc_refs[pg][0, 0, h]
                s_ref[h, c, :, pg * PAGE_SIZE:(pg + 1) * PAGE_SIZE] = _nt_dot(qs, kp.astype(_bf16))
                sums.append(jnp.sum(kp, axis=0, keepdims=True))
            ppb = MOBA_BLOCK // PAGE_SIZE
            bs = [functools.reduce(lambda a, b: a + b, sums[j * ppb:(j + 1) * ppb]) for j in range(P // ppb)]
            off = pl.multiple_of(c * BLOCKS_PER_STEP, BLOCKS_PER_STEP)
            ksul_ref[h, pl.ds(off, BLOCKS_PER_STEP), :] = jnp.concatenate(bs, axis=0)

    @pl.when(c == nc - 1)
    def _():
        for h in range(N_KV_HEADS):
            q = stacked_q(h)
            kmean = ksul_ref[h] / MOBA_BLOCK
            gate = _nt_dot(q, kmean, precision=lax.Precision.HIGHEST)
            jidx = lax.broadcasted_iota(jnp.int32, (QR, AUX_COLS), 1)
            sel = _topk_select(gate, jidx < nblk, jidx, nblk, axis=1)
            slope = slope_ref[h][:, 0:1]
            bias = jnp.where(sel, slope * (MOBA_BLOCK * (jidx - nblk)).astype(_f32), MASK_NEG)
            aux = jnp.where(jidx < nblk, bias, jnp.where(jidx == nblk, slope, 0.0)).astype(_bf16)
            m = jnp.full((QR, 1), MASK_NEG, _f32)
            for cc in range(nc):
                sc = s_ref[h, cc] + _dot(aux, expand_ref[cc])
                s_ref[h, cc] = sc
                m = jnp.maximum(m, jnp.max(sc, axis=-1, keepdims=True))
            kn = kn_ref[0, h]
            so = _nt_dot((q * ATTN_SCALE).astype(_bf16), kn.astype(_bf16))
            tt = lax.broadcasted_iota(jnp.int32, (QR, ds), 0) % ds
            ss = lax.broadcasted_iota(jnp.int32, (QR, ds), 1)
            so = jnp.where(ss <= tt, so + slope * ss.astype(_f32), MASK_NEG)
            m = jnp.maximum(m, jnp.max(so, axis=-1, keepdims=True))
            po = jnp.exp(so - m)
            l = jnp.sum(po, axis=-1, keepdims=True)
            for cc in range(nc):
                p = jnp.exp(s_ref[h, cc] - m)
                l = l + jnp.sum(p, axis=-1, keepdims=True)
                p_ref[h, cc] = p.astype(_bf16)
            acc_ref[h] = _dot(po.astype(_bf16), vn_ref[0, h].astype(_bf16))
            l_ref[h] = jnp.broadcast_to(l, (QR, LANES))

    @pl.when(c >= nc)
    def _():
        cv = c - nc
        for h in range(N_KV_HEADS):
            acc = acc_ref[h]
            p = p_ref[h, cv]
            for pg in range(P):
                vp = vc_refs[pg][0, 0, h].astype(_bf16)
                acc = acc + _dot(p[:, pg * PAGE_SIZE:(pg + 1) * PAGE_SIZE], vp)
            acc_ref[h] = acc

    @pl.when(c == 2 * nc - 1)
    def _():
        for h in range(N_KV_HEADS):
            o = acc_ref[h] / l_ref[h][:, 0:1]
            for g in range(KV_GROUP):
                hh = KV_GROUP * h + g
                o_ref[0, :, hh * HEAD_DIM:(hh + 1) * HEAD_DIM] = o[g * ds:(g + 1) * ds]


def _attn_sample(q, k_new, v_new, cache_k, cache_v, page_table, layer):
    DB, DS, _ = q.shape
    n_pages = page_table.shape[1]
    past = n_pages * PAGE_SIZE
    assert past % MOBA_BLOCK == 0 and DS <= MOBA_BLOCK and DS % SUBLANES == 0
    nblk = past // MOBA_BLOCK
    assert nblk >= MOBA_TOPK and nblk + SUBLANES <= AUX_COLS
    P = PAGES_PER_STEP
    assert n_pages % P == 0 and BLOCKS_PER_STEP % SUBLANES == 0
    nc = n_pages // P
    QR = KV_GROUP * DS
    CW = P * PAGE_SIZE

    slopes = 2.0 ** (-8.0 * jnp.arange(1, N_HEADS + 1, dtype=_f32) / N_HEADS)
    slope_tab = jnp.broadcast_to(
        jnp.repeat(slopes.reshape(N_KV_HEADS, KV_GROUP), DS, axis=1)[:, :, None], (N_KV_HEADS, QR, LANES))
    key = jnp.arange(past)
    onehot = (jnp.arange(AUX_COLS)[:, None] == (key // MOBA_BLOCK)[None, :]).astype(_f32)
    rrow = jnp.where(jnp.arange(AUX_COLS)[:, None] == nblk, (key % MOBA_BLOCK)[None, :].astype(_f32), 0.0)
    expand = (onehot + rrow).reshape(AUX_COLS, nc, CW).transpose(1, 0, 2).astype(_bf16)

    def k_map(pg):
        def f(b, c, pt):
            nxt = jnp.minimum(b + 1, DB - 1)
            page = jnp.where(c < nc, pt[b, jnp.minimum(c, nc - 1) * P + pg], pt[nxt, pg])
            return (layer, page, 0, 0, 0)
        return f

    def v_map(pg):
        def f(b, c, pt):
            page = pt[b, jnp.maximum(c - nc, 0) * P + pg]
            return (layer, page, 0, 0, 0)
        return f

    page_block = (1, 1, N_KV_HEADS, PAGE_SIZE, HEAD_DIM)
    tok_spec = pl.BlockSpec((1, DS, Q_COLS), lambda b, c, pt: (b, 0, 0))
    new_spec = pl.BlockSpec((1, N_KV_HEADS, DS, HEAD_DIM), lambda b, c, pt: (b, 0, 0, 0))
    in_specs = [tok_spec, new_spec, new_spec,
                pl.BlockSpec((N_KV_HEADS, QR, LANES), lambda b, c, pt: (0, 0, 0)),
                pl.BlockSpec((nc, AUX_COLS, CW), lambda b, c, pt: (0, 0, 0))]
    in_specs += [pl.BlockSpec(page_block, k_map(pg)) for pg in range(P)]
    in_specs += [pl.BlockSpec(page_block, v_map(pg)) for pg in range(P)]

    return pl.pallas_call(
        functools.partial(_attn_sample_kernel, nc=nc, ds=DS, nblk=nblk),
        out_shape=jax.ShapeDtypeStruct((DB, DS, Q_COLS), _f32),
        grid_spec=pltpu.PrefetchScalarGridSpec(
            num_scalar_prefetch=1,
            grid=(DB, 2 * nc),
            in_specs=in_specs,
            out_specs=tok_spec,
            scratch_shapes=[pltpu.VMEM((N_KV_HEADS, nc, QR, CW), _f32),
                            pltpu.VMEM((N_KV_HEADS, nc, QR, CW), _bf16),
                            pltpu.VMEM((N_KV_HEADS, AUX_COLS, HEAD_DIM), _f32),
                            pltpu.VMEM((N_KV_HEADS, QR, HEAD_DIM), _f32),
                            pltpu.VMEM((N_KV_HEADS, QR, LANES), _f32)]),
        compiler_params=pltpu.CompilerParams(
            dimension_semantics=("arbitrary", "arbitrary"), vmem_limit_bytes=VMEM_LIMIT_BYTES),
        name="attn_sample",
    )(page_table, q, k_new, v_new, slope_tab, expand, *([cache_k] * P), *([cache_v] * P))


def _pool_windows_kernel(uext_ref, d_ref, *, ds, first_pos):
    cur = uext_ref[:, HIST_ROWS:HIST_ROWS + ds, :]
    pos = first_pos + lax.broadcasted_iota(jnp.int32, (1, ds, 1), 1)
    for g, w in enumerate(POOL_WINDOWS):
        sl = slice(g * POOL_GROUP_DIM, (g + 1) * POOL_GROUP_DIM)
        win = cur[:, :, sl]
        for dd in range(1, w):
            win = win + uext_ref[:, HIST_ROWS - dd:HIST_ROWS - dd + ds, sl]
        cnt = jnp.minimum(w, pos + 1).astype(_f32)
        d_ref[:, :, sl] = win / cnt - cur[:, :, sl]


def _pool_windows(u_ext_padded, ds, first_pos):
    DB, R, P = u_ext_padded.shape
    assert R == HIST_ROWS + ds
    bb = 16
    assert DB % bb == 0
    return pl.pallas_call(
        functools.partial(_pool_windows_kernel, ds=ds, first_pos=first_pos),
        out_shape=jax.ShapeDtypeStruct((DB, ds, P), _f32),
        grid=(DB // bb,),
        in_specs=[pl.BlockSpec((bb, R, P), lambda b: (b, 0, 0))],
        out_specs=pl.BlockSpec((bb, ds, P), lambda b: (b, 0, 0)),
        compiler_params=pltpu.CompilerParams(dimension_semantics=("arbitrary",)),
        name="pool_windows",
    )(u_ext_padded)


def _merge_ffn_kernel(x_ref, attn_ref, d_ref, ga_ref, gb_ref, wpool_ref, pscale_ref, wout_ref, ln2_ref,
                      wgate_ref, wup_ref, wdown_ref, y_ref):
    d = d_ref[...]
    pool = jnp.concatenate(
        [_dot(d[:, g * POOL_GROUP_DIM:(g + 1) * POOL_GROUP_DIM].astype(_bf16), wpool_ref[g])
         for g in range(N_POOL_GROUPS)], axis=1) * pscale_ref[...]
    h = _sigmoid(ga_ref[...]) * attn_ref[...] + _sigmoid(gb_ref[...]) * pool
    x1 = x_ref[...] + _dot(h.astype(_bf16), wout_ref[...])
    hn = _rms(x1, ln2_ref[...]).astype(_bf16)
    gate = _dot(hn, wgate_ref[...])
    up = _dot(hn, wup_ref[...])
    act = (gate * _sigmoid(gate) * up).astype(_bf16)
    y_ref[...] = x1 + _dot(act, wdown_ref[...])


def _merge_ffn(x, attn, d, ga, gb, w_pool, pool_scale, w_out, ln2_g, w_gate, w_up, w_down):
    N, D = x.shape
    tm = min(TOKEN_TILE, N)
    assert N % tm == 0 and tm % SUBLANES == 0
    d_ff = w_gate.shape[1]
    tok = pl.BlockSpec((tm, D), lambda t: (t, 0))
    return pl.pallas_call(
        _merge_ffn_kernel,
        out_shape=jax.ShapeDtypeStruct((N, D), _f32),
        grid=(N // tm,),
        in_specs=[tok, tok, tok, tok, tok,
                  _resident((N_POOL_GROUPS, POOL_GROUP_DIM, POOL_GROUP_DIM)), _resident((1, POOL_WIDTH)),
                  _resident((D, D)), _resident((1, D)),
                  _resident((D, d_ff)), _resident((D, d_ff)), _resident((d_ff, D))],
        out_specs=tok,
        compiler_params=pltpu.CompilerParams(
            dimension_semantics=("arbitrary",), vmem_limit_bytes=VMEM_LIMIT_BYTES),
        name="merge_ffn",
    )(x, attn, d, ga, gb, w_pool, pool_scale.reshape(1, POOL_WIDTH), w_out, ln2_g.reshape(1, D),
      w_gate, w_up, w_down)


def kernel(x_prompt, x_sample, cache_k, cache_v, state_pool, page_table, ln1_g, w_in, q_norm_g, k_norm_g,
           w_pool, pool_scale, w_out, ln2_g, w_gate, w_up, w_down):
    depth = w_in.shape[0]
    B, S, D = x_prompt.shape
    DB, DS, _ = x_sample.shape
    past_len = page_table.shape[1] * PAGE_SIZE
    x_p, x_s = x_prompt, x_sample
    kp_l, vp_l, pp_l, ks_l, vs_l, ps_l = [], [], [], [], [], []
    for l in range(depth):
        w_in_b, w_pool_b, w_out_b = w_in[l].astype(_bf16), w_pool[l].astype(_bf16), w_out[l].astype(_bf16)
        w_gate_b, w_up_b, w_down_b = w_gate[l].astype(_bf16), w_up[l].astype(_bf16), w_down[l].astype(_bf16)
        ffn = functools.partial(_merge_ffn, w_pool=w_pool_b, pool_scale=pool_scale[l], w_out=w_out_b,
                                ln2_g=ln2_g[l], w_gate=w_gate_b, w_up=w_up_b, w_down=w_down_b)

        q, k, v, d, ga, gb, hist = _proj(x_p, ln1_g[l], w_in_b, q_norm_g[l], k_norm_g[l], windows=True)
        attn = _attn_prompt(q, k, v)
        flat = lambda a: a.reshape(B * S, a.shape[-1])
        x_p = ffn(flat(x_p), flat(attn), flat(d), flat(ga), flat(gb)).reshape(B, S, D)
        kp_l.append(k)
        vp_l.append(v)
        pp_l.append(hist[:, HIST_ROWS - POOL_HIST:])

        q, k, v, u, ga, gb = _proj(x_s.reshape(1, DB * DS, D), ln1_g[l], w_in_b, q_norm_g[l], k_norm_g[l],
                                   windows=False)
        per_seq = lambda a: a.reshape(N_KV_HEADS, DB, DS, HEAD_DIM).transpose(1, 0, 2, 3)
        k, v = per_seq(k), per_seq(v)
        attn = _attn_sample(q.reshape(DB, DS, Q_COLS), k, v, cache_k, cache_v, page_table, l)
        u_ext = jnp.concatenate(
            [jnp.zeros((DB, HIST_ROWS - POOL_HIST, POOL_WIDTH), _f32), state_pool[l], u.reshape(DB, DS, POOL_WIDTH)],
            axis=1)
        d = _pool_windows(u_ext, DS, past_len)
        flat = lambda a: a.reshape(DB * DS, a.shape[-1])
        x_s = ffn(flat(x_s), flat(attn), flat(d), flat(ga), flat(gb)).reshape(DB, DS, D)
        ks_l.append(k)
        vs_l.append(v)
        ps_l.append(u_ext[:, HIST_ROWS + DS - POOL_HIST:])
    return (x_p, x_s, jnp.stack(kp_l), jnp.stack(vp_l), jnp.stack(pp_l),
            jnp.stack(ks_l), jnp.stack(vs_l), jnp.stack(ps_l))
```

```python
import functools

import jax
import jax.numpy as jnp
from jax import lax
from jax.experimental import pallas as pl
from jax.experimental.pallas import tpu as pltpu

D_MODEL = 1024
N_HEADS = 8
HEAD_DIM = D_MODEL // N_HEADS
N_KV_HEADS = 4
KV_GROUP = N_HEADS // N_KV_HEADS
MOBA_BLOCK = 256
MOBA_TOPK = 3
PAGE_SIZE = 128
ATTN_SCALE = HEAD_DIM ** -0.5
POOL_WINDOWS = (2, 4, 8, 16)
N_POOL_GROUPS = len(POOL_WINDOWS)
POOL_WIDTH = D_MODEL
POOL_GROUP_DIM = POOL_WIDTH // N_POOL_GROUPS
POOL_HIST = max(POOL_WINDOWS) - 1
RMS_EPS = 1e-6
Q_COLS = N_HEADS * HEAD_DIM
KV_COLS = N_KV_HEADS * HEAD_DIM
COL_Q = 0
COL_K = Q_COLS
COL_V = COL_K + KV_COLS
COL_U = COL_V + KV_COLS
COL_GA = COL_U + POOL_WIDTH
COL_GB = COL_GA + D_MODEL
IN_COLS = COL_GB + D_MODEL

SUBLANES = 8
LANES = 128
MXU_DIM = 256
VMEM_LIMIT_BYTES = 56 * 1024 * 1024

HIST_ROWS = 16
assert HIST_ROWS >= POOL_HIST and HIST_ROWS % SUBLANES == 0
MASK_NEG = -(2.0 ** 30)

TOKEN_TILE = 256
PAGES_PER_STEP = 16
BLOCKS_PER_STEP = PAGES_PER_STEP * PAGE_SIZE // MOBA_BLOCK

_f32 = jnp.float32
_bf16 = jnp.bfloat16


def _nt_dot(a, b, precision=None):
    return lax.dot_general(a, b, (((1,), (1,)), ((), ())), precision=precision,
                           preferred_element_type=_f32)


def _dot(a, b):
    return jnp.dot(a, b, preferred_element_type=_f32)


def _rms(x, g):
    return x * lax.rsqrt(jnp.mean(x * x, axis=-1, keepdims=True) + RMS_EPS) * g


def _sigmoid(x):
    return 1.0 / (1.0 + jnp.exp(-x))


def _resident(shape):
    nd = len(shape)
    return pl.BlockSpec(shape, lambda *_: (0,) * nd, pipeline_mode=pl.Buffered(1))


def _proj_kernel(x_ref, ln1_ref, w_ref, qg_ref, kg_ref,
                 q_ref, k_ref, v_ref, u_ref, ga_ref, gb_ref, *rest, tm, windows):
    xn = _rms(x_ref[0], ln1_ref[...]).astype(_bf16)

    zq = _dot(xn, w_ref[:, COL_Q:COL_K])
    qg = qg_ref[...]
    for h in range(N_HEADS):
        sl = slice(h * HEAD_DIM, (h + 1) * HEAD_DIM)
        q_ref[0, :, sl] = _rms(zq[:, sl], qg)

    zk = _dot(xn, w_ref[:, COL_K:COL_V])
    zv = _dot(xn, w_ref[:, COL_V:COL_U])
    kg = kg_ref[...]
    for h in range(N_KV_HEADS):
        sl = slice(h * HEAD_DIM, (h + 1) * HEAD_DIM)
        k_ref[0, h] = _rms(zk[:, sl], kg)
        v_ref[0, h] = zv[:, sl]

    ga_ref[0] = _dot(xn, w_ref[:, COL_GA:COL_GB])
    gb_ref[0] = _dot(xn, w_ref[:, COL_GB:IN_COLS])

    zu = _dot(xn, w_ref[:, COL_U:COL_GA])
    if not windows:
        u_ref[0] = zu
        return

    hist_ref, uext_ref = rest
    si = pl.program_id(1)

    @pl.when(si == 0)
    def _():
        uext_ref[0:HIST_ROWS, :] = jnp.zeros((HIST_ROWS, POOL_WIDTH), _f32)

    uext_ref[HIST_ROWS:HIST_ROWS + tm, :] = zu
    pos = si * tm + lax.broadcasted_iota(jnp.int32, (tm, 1), 0)
    for g, w in enumerate(POOL_WINDOWS):
        sl = slice(g * POOL_GROUP_DIM, (g + 1) * POOL_GROUP_DIM)
        win = zu[:, sl]
        for dd in range(1, w):
            win = win + uext_ref[HIST_ROWS - dd:HIST_ROWS - dd + tm, sl]
        cnt = jnp.minimum(w, pos + 1).astype(_f32)
        u_ref[0, :, sl] = win / cnt - zu[:, sl]
    hist_ref[0] = zu[tm - HIST_ROWS:tm, :]
    uext_ref[0:HIST_ROWS, :] = zu[tm - HIST_ROWS:tm, :]


def _proj(x, ln1_g, w_in_bf16, q_norm_g, k_norm_g, *, windows):
    B, S, D = x.shape
    tm = min(TOKEN_TILE, S)
    assert S % tm == 0 and tm % HIST_ROWS == 0
    tok = lambda n: pl.BlockSpec((1, tm, n), lambda b, s: (b, s, 0))
    kv = pl.BlockSpec((1, N_KV_HEADS, tm, HEAD_DIM), lambda b, s: (b, 0, s, 0))
    out_shape = [
        jax.ShapeDtypeStruct((B, S, Q_COLS), _f32),
        jax.ShapeDtypeStruct((B, N_KV_HEADS, S, HEAD_DIM), _f32),
        jax.ShapeDtypeStruct((B, N_KV_HEADS, S, HEAD_DIM), _f32),
        jax.ShapeDtypeStruct((B, S, POOL_WIDTH), _f32),
        jax.ShapeDtypeStruct((B, S, D_MODEL), _f32),
        jax.ShapeDtypeStruct((B, S, D_MODEL), _f32),
    ]
    out_specs = [tok(Q_COLS), kv, kv, tok(POOL_WIDTH), tok(D_MODEL), tok(D_MODEL)]
    scratch = []
    if windows:
        out_shape.append(jax.ShapeDtypeStruct((B, HIST_ROWS, POOL_WIDTH), _f32))
        out_specs.append(pl.BlockSpec((1, HIST_ROWS, POOL_WIDTH), lambda b, s: (b, 0, 0)))
        scratch.append(pltpu.VMEM((HIST_ROWS + tm, POOL_WIDTH), _f32))
    return pl.pallas_call(
        functools.partial(_proj_kernel, tm=tm, windows=windows),
        out_shape=out_shape,
        grid=(B, S // tm),
        in_specs=[tok(D), _resident((1, D)), _resident((D, IN_COLS)),
                  _resident((1, HEAD_DIM)), _resident((1, HEAD_DIM))],
        out_specs=out_specs,
        scratch_shapes=scratch,
        compiler_params=pltpu.CompilerParams(
            dimension_semantics=("arbitrary", "arbitrary"), vmem_limit_bytes=VMEM_LIMIT_BYTES),
        name="proj_windows" if windows else "proj",
    )(x, ln1_g.reshape(1, D), w_in_bf16, q_norm_g.reshape(1, HEAD_DIM), k_norm_g.reshape(1, HEAD_DIM))


def _topk_select(g, valid, row_idx, n_rows, axis):
    gm = jnp.where(valid, g, -jnp.inf)
    cnt = jnp.zeros(g.shape, jnp.int32)
    for jp in range(n_rows):
        gj = lax.slice_in_dim(gm, jp, jp + 1, axis=axis)
        beats = (gj > gm) | ((gj == gm) & (jp < row_idx))
        cnt = cnt + jnp.where(beats, 1, 0)
    return valid & (cnt < MOBA_TOPK)


def _split_bf16(x):
    hi = x.astype(_bf16)
    return hi, (x - hi.astype(_f32)).astype(_bf16)


def _attn_prompt_kernel(q_ref, k_ref, v_ref, slope_ref, o_ref, kaug_ref, vt_ref, *, nb, nbp):
    S = nb * MOBA_BLOCK
    QR = KV_GROUP * MOBA_BLOCK
    KM_ROWS = 2 * SUBLANES

    k = k_ref[0, 0]
    kaug_ref[:, 0:HEAD_DIM] = k.astype(_bf16)
    row = lax.broadcasted_iota(jnp.int32, (S, LANES), 0)
    lane = lax.broadcasted_iota(jnp.int32, (S, LANES), 1)
    onehot = jnp.where(lane == row // MOBA_BLOCK, 1.0, 0.0)
    rpos = jnp.where(lane == nbp, (row % MOBA_BLOCK).astype(_f32), 0.0)
    kaug_ref[:, HEAD_DIM:HEAD_DIM + LANES] = (onehot + rpos).astype(_bf16)
    vt_ref[...] = v_ref[0, 0].T.astype(_bf16)

    kmean = jnp.mean(k.reshape(nb, MOBA_BLOCK, HEAD_DIM), axis=1)
    km = jnp.concatenate([kmean, jnp.zeros((KM_ROWS - nb, HEAD_DIM), _f32)], axis=0)
    km_hi, km_lo = _split_bf16(km)
    km_cat = jnp.concatenate([km_hi, km_hi, km_lo], axis=1)

    slope_rows = slope_ref[0]
    slope = slope_rows[0:1, :]
    jidx = lax.broadcasted_iota(jnp.int32, (nbp, QR), 0)
    key_r = lax.broadcasted_iota(jnp.int32, (MOBA_BLOCK, QR), 0)
    t_rel = lax.broadcasted_iota(jnp.int32, (MOBA_BLOCK, QR), 1) % MOBA_BLOCK
    causal = key_r <= t_rel

    for i in range(nb):
        rows = slice(i * MOBA_BLOCK, (i + 1) * MOBA_BLOCK)
        q2 = q_ref[0, rows, :]
        q_t = jnp.concatenate([q2[:, h * HEAD_DIM:(h + 1) * HEAD_DIM].T for h in range(KV_GROUP)], axis=1)

        if i > MOBA_TOPK:
            q_hi, q_lo = _split_bf16(q_t)
            gate_t = _dot(km_cat, jnp.concatenate([q_hi, q_lo, q_hi], axis=0))[0:nbp]
            sel = _topk_select(gate_t, jidx < i, jidx, i, axis=0)
        else:
            sel = jidx < i
        bias_t = jnp.where(sel, slope * (MOBA_BLOCK * (jidx - i)).astype(_f32), MASK_NEG)
        bias_t = jnp.where(jidx == i, 0.0, bias_t)
        aux_t = jnp.concatenate(
            [bias_t, slope_rows, jnp.zeros((LANES - nbp - SUBLANES, QR), _f32)], axis=0)
        qa_t = jnp.concatenate([(q_t * ATTN_SCALE).astype(_bf16), aux_t.astype(_bf16)], axis=0)

        n = (i + 1) * MOBA_BLOCK
        s = _dot(kaug_ref[0:n, :], qa_t)
        own = jnp.where(causal, s[i * MOBA_BLOCK:n], MASK_NEG)
        s = own if i == 0 else jnp.concatenate([s[0:i * MOBA_BLOCK], own], axis=0)
        m = jnp.max(s, axis=0, keepdims=True)
        p = jnp.exp(s - m)
        l = jnp.sum(p, axis=0, keepdims=True)
        o_t = _dot(vt_ref[:, 0:n], p.astype(_bf16)) / l
        o_ref[0, rows, :] = jnp.concatenate(
            [o_t[:, h * MOBA_BLOCK:(h + 1) * MOBA_BLOCK].T for h in range(KV_GROUP)], axis=1)


def _slope_table(rows_per_head):
    slopes = 2.0 ** (-8.0 * jnp.arange(1, N_HEADS + 1, dtype=_f32) / N_HEADS)
    per_row = jnp.repeat(slopes.reshape(N_KV_HEADS, KV_GROUP), rows_per_head, axis=1)
    tab = jnp.zeros((N_KV_HEADS, SUBLANES, KV_GROUP * rows_per_head), _f32)
    return tab.at[:, 0, :].set(per_row)


def _attn_prompt(q, k, v):
    B, S, _ = q.shape
    assert S % MOBA_BLOCK == 0
    nb = S // MOBA_BLOCK
    nbp = -(-nb // SUBLANES) * SUBLANES
    assert nbp + SUBLANES <= LANES
    gw = KV_GROUP * HEAD_DIM
    QR = KV_GROUP * MOBA_BLOCK
    assert nb <= 2 * SUBLANES
    qo_spec = pl.BlockSpec((1, S, gw), lambda b, h: (b, 0, h))
    kv_spec = pl.BlockSpec((1, 1, S, HEAD_DIM), lambda b, h: (b, h, 0, 0))
    return pl.pallas_call(
        functools.partial(_attn_prompt_kernel, nb=nb, nbp=nbp),
        out_shape=jax.ShapeDtypeStruct((B, S, Q_COLS), _f32),
        grid=(B, N_KV_HEADS),
        in_specs=[qo_spec, kv_spec, kv_spec,
                  pl.BlockSpec((1, SUBLANES, QR), lambda b, h: (h, 0, 0))],
        out_specs=qo_spec,
        scratch_shapes=[pltpu.VMEM((S, HEAD_DIM + LANES), _bf16),
                        pltpu.VMEM((HEAD_DIM, S), _bf16)],
        compiler_params=pltpu.CompilerParams(
            dimension_semantics=("arbitrary", "arbitrary"),
            vmem_limit_bytes=VMEM_LIMIT_BYTES),
        name="attn_prompt",
    )(q, k, v, _slope_table(MOBA_BLOCK))


def _attn_sample_kernel(pt_ref, q_ref, kn_ref, vn_ref, slope_row_ref, slope_col_ref, expand_ref, *rest,
                        nc, ds, nblk):
    del pt_ref
    P = PAGES_PER_STEP
    kc_refs, vc_refs = rest[:P], rest[P:2 * P]
    o_ref = rest[2 * P]
    s_ref, p_ref, ksum_ref, acc_ref, l_ref = rest[2 * P + 1:]
    c = pl.program_id(1)
    QR = KV_GROUP * ds
    R = N_KV_HEADS * QR

    def stacked_q(h):
        q = q_ref[0]
        return jnp.concatenate(
            [q[:, (KV_GROUP * h + g) * HEAD_DIM:(KV_GROUP * h + g + 1) * HEAD_DIM] for g in range(KV_GROUP)],
            axis=0)

    @pl.when(c == 0)
    def _():
        ksum_ref[...] = jnp.zeros(ksum_ref.shape, _f32)

    @pl.when(c < nc)
    def _():
        for h in range(N_KV_HEADS):
            qs = (stacked_q(h) * ATTN_SCALE).astype(_bf16)
            sums = []
            for pg in range(P):
                kp = kc_refs[pg][0, 0, h]
                s_ref[c, h * QR:(h + 1) * QR, pg * PAGE_SIZE:(pg + 1) * PAGE_SIZE] = _nt_dot(qs, kp.astype(_bf16))
                sums.append(jnp.sum(kp, axis=0, keepdims=True))
            ppb = MOBA_BLOCK // PAGE_SIZE
            bs = [functools.reduce(lambda a, b: a + b, sums[j * ppb:(j + 1) * ppb]) for j in range(P // ppb)]
            off = pl.multiple_of(c * BLOCKS_PER_STEP, BLOCKS_PER_STEP)
            ksum_ref[h, pl.ds(off, BLOCKS_PER_STEP), :] = jnp.concatenate(bs, axis=0)

    @pl.when(c == nc - 1)
    def _():
        qs = [stacked_q(h) for h in range(N_KV_HEADS)]
        gate = jnp.concatenate(
            [_nt_dot(qs[h], ksum_ref[h] / MOBA_BLOCK, precision=lax.Precision.HIGHEST) for h in range(N_KV_HEADS)]
            + [jnp.zeros((LANES - R, LANES), _f32)], axis=0)
        g_t = gate.T[0:nblk]
        jidx = lax.broadcasted_iota(jnp.int32, (nblk, LANES), 0)
        sel = _topk_select(g_t, jidx < nblk, jidx, nblk, axis=0)
        slope_rows = slope_row_ref[...]
        bias_t = jnp.where(sel, slope_rows[0:1, :] * (MOBA_BLOCK * (jidx - nblk)).astype(_f32), MASK_NEG)
        aux_t = jnp.concatenate(
            [bias_t, slope_rows, jnp.zeros((LANES - nblk - SUBLANES, LANES), _f32)], axis=0)
        aux = aux_t.T[0:R].astype(_bf16)

        m = jnp.full((R, 1), MASK_NEG, _f32)
        for cc in range(nc):
            sc = s_ref[cc] + _dot(aux, expand_ref[cc])
            s_ref[cc] = sc
            m = jnp.maximum(m, jnp.max(sc, axis=-1, keepdims=True))
        so = jnp.concatenate(
            [_nt_dot((qs[h] * ATTN_SCALE).astype(_bf16), kn_ref[0, h].astype(_bf16)) for h in range(N_KV_HEADS)],
            axis=0)
        tt = lax.broadcasted_iota(jnp.int32, (R, ds), 0) % ds
        ss = lax.broadcasted_iota(jnp.int32, (R, ds), 1)
        so = jnp.where(ss <= tt, so + slope_col_ref[:, 0:1] * ss.astype(_f32), MASK_NEG)
        m = jnp.maximum(m, jnp.max(so, axis=-1, keepdims=True))
        po = jnp.exp(so - m)
        l = jnp.sum(po, axis=-1, keepdims=True)
        for cc in range(nc):
            p = jnp.exp(s_ref[cc] - m)
            l = l + jnp.sum(p, axis=-1, keepdims=True)
            p_ref[cc] = p.astype(_bf16)
        for h in range(N_KV_HEADS):
            acc_ref[h] = _dot(po[h * QR:(h + 1) * QR].astype(_bf16), vn_ref[0, h].astype(_bf16))
        l_ref[...] = jnp.broadcast_to(l, (R, LANES))

    @pl.when(c >= nc)
    def _():
        p = p_ref[c - nc]
        for h in range(N_KV_HEADS):
            acc = acc_ref[h]
            for pg in range(P):
                vp = vc_refs[pg][0, 0, h].astype(_bf16)
                acc = acc + _dot(p[h * QR:(h + 1) * QR, pg * PAGE_SIZE:(pg + 1) * PAGE_SIZE], vp)
            acc_ref[h] = acc

    @pl.when(c == 2 * nc - 1)
    def _():
        for h in range(N_KV_HEADS):
            o = acc_ref[h] / l_ref[h * QR:(h + 1) * QR, 0:1]
            for g in range(KV_GROUP):
                hh = KV_GROUP * h + g
                o_ref[0, :, hh * HEAD_DIM:(hh + 1) * HEAD_DIM] = o[g * ds:(g + 1) * ds]


def _attn_sample(q, k_new, v_new, cache_k, cache_v, page_table, layer):
    DB, DS, _ = q.shape
    n_pages = page_table.shape[1]
    past = n_pages * PAGE_SIZE
    assert past % MOBA_BLOCK == 0 and DS <= MOBA_BLOCK and DS % SUBLANES == 0
    nblk = past // MOBA_BLOCK
    assert nblk >= MOBA_TOPK and nblk % SUBLANES == 0 and nblk + SUBLANES <= LANES
    P = PAGES_PER_STEP
    assert n_pages % P == 0 and BLOCKS_PER_STEP % SUBLANES == 0
    nc = n_pages // P
    QR = KV_GROUP * DS
    R = N_KV_HEADS * QR
    assert R <= LANES and QR % (2 * SUBLANES) == 0
    CW = P * PAGE_SIZE

    slopes = 2.0 ** (-8.0 * jnp.arange(1, N_HEADS + 1, dtype=_f32) / N_HEADS)
    per_row = jnp.repeat(slopes, DS)
    slope_row = jnp.zeros((SUBLANES, LANES), _f32).at[0, :R].set(per_row)
    slope_col = jnp.broadcast_to(per_row[:, None], (R, LANES))
    key = jnp.arange(past)
    feat = jnp.arange(LANES)[:, None]
    expand = jnp.where(feat == nblk, (key % MOBA_BLOCK)[None, :].astype(_f32),
                       (feat == (key // MOBA_BLOCK)[None, :]).astype(_f32))
    expand = expand.reshape(LANES, nc, CW).transpose(1, 0, 2).astype(_bf16)

    def k_map(pg):
        def f(b, c, pt):
            nxt = jnp.minimum(b + 1, DB - 1)
            page = jnp.where(c < nc, pt[b, jnp.minimum(c, nc - 1) * P + pg], pt[nxt, pg])
            return (layer, page, 0, 0, 0)
        return f

    def v_map(pg):
        def f(b, c, pt):
            page = pt[b, jnp.maximum(c - nc, 0) * P + pg]
            return (layer, page, 0, 0, 0)
        return f

    page_block = (1, 1, N_KV_HEADS, PAGE_SIZE, HEAD_DIM)
    tok_spec = pl.BlockSpec((1, DS, Q_COLS), lambda b, c, pt: (b, 0, 0))
    new_spec = pl.BlockSpec((1, N_KV_HEADS, DS, HEAD_DIM), lambda b, c, pt: (b, 0, 0, 0))
    in_specs = [tok_spec, new_spec, new_spec,
                pl.BlockSpec((SUBLANES, LANES), lambda b, c, pt: (0, 0)),
                pl.BlockSpec((R, LANES), lambda b, c, pt: (0, 0)),
                pl.BlockSpec((nc, LANES, CW), lambda b, c, pt: (0, 0, 0))]
    in_specs += [pl.BlockSpec(page_block, k_map(pg)) for pg in range(P)]
    in_specs += [pl.BlockSpec(page_block, v_map(pg)) for pg in range(P)]

    return pl.pallas_call(
        functools.partial(_attn_sample_kernel, nc=nc, ds=DS, nblk=nblk),
        out_shape=jax.ShapeDtypeStruct((DB, DS, Q_COLS), _f32),
        grid_spec=pltpu.PrefetchScalarGridSpec(
            num_scalar_prefetch=1,
            grid=(DB, 2 * nc),
            in_specs=in_specs,
            out_specs=tok_spec,
            scratch_shapes=[pltpu.VMEM((nc, R, CW), _f32),
                            pltpu.VMEM((nc, R, CW), _bf16),
                            pltpu.VMEM((N_KV_HEADS, LANES, HEAD_DIM), _f32),
                            pltpu.VMEM((N_KV_HEADS, QR, HEAD_DIM), _f32),
                            pltpu.VMEM((R, LANES), _f32)]),
        compiler_params=pltpu.CompilerParams(
            dimension_semantics=("arbitrary", "arbitrary"), vmem_limit_bytes=VMEM_LIMIT_BYTES),
        name="attn_sample",
    )(page_table, q, k_new, v_new, slope_row, slope_col, expand, *([cache_k] * P), *([cache_v] * P))


def _pool_windows_kernel(uext_ref, d_ref, *, ds, first_pos):
    cur = uext_ref[:, HIST_ROWS:HIST_ROWS + ds, :]
    pos = first_pos + lax.broadcasted_iota(jnp.int32, (1, ds, 1), 1)
    for g, w in enumerate(POOL_WINDOWS):
        sl = slice(g * POOL_GROUP_DIM, (g + 1) * POOL_GROUP_DIM)
        win = cur[:, :, sl]
        for dd in range(1, w):
            win = win + uext_ref[:, HIST_ROWS - dd:HIST_ROWS - dd + ds, sl]
        cnt = jnp.minimum(w, pos + 1).astype(_f32)
        d_ref[:, :, sl] = win / cnt - cur[:, :, sl]


def _pool_windows(u_ext_padded, ds, first_pos):
    DB, R, P = u_ext_padded.shape
    assert R == HIST_ROWS + ds
    bb = 16
    assert DB % bb == 0
    return pl.pallas_call(
        functools.partial(_pool_windows_kernel, ds=ds, first_pos=first_pos),
        out_shape=jax.ShapeDtypeStruct((DB, ds, P), _f32),
        grid=(DB // bb,),
        in_specs=[pl.BlockSpec((bb, R, P), lambda b: (b, 0, 0))],
        out_specs=pl.BlockSpec((bb, ds, P), lambda b: (b, 0, 0)),
        compiler_params=pltpu.CompilerParams(dimension_semantics=("arbitrary",)),
        name="pool_windows",
    )(u_ext_padded)


def _merge_ffn_kernel(x_ref, attn_ref, d_ref, ga_ref, gb_ref, wpool_ref, pscale_ref, wout_ref, ln2_ref,
                      wgate_ref, wup_ref, wdown_ref, y_ref):
    d = d_ref[...]
    pool = jnp.concatenate(
        [_dot(d[:, g * POOL_GROUP_DIM:(g + 1) * POOL_GROUP_DIM].astype(_bf16), wpool_ref[g])
         for g in range(N_POOL_GROUPS)], axis=1) * pscale_ref[...]
    h = _sigmoid(ga_ref[...]) * attn_ref[...] + _sigmoid(gb_ref[...]) * pool
    x1 = x_ref[...] + _dot(h.astype(_bf16), wout_ref[...])
    hn = _rms(x1, ln2_ref[...]).astype(_bf16)
    gate = _dot(hn, wgate_ref[...])
    up = _dot(hn, wup_ref[...])
    act = (gate * _sigmoid(gate) * up).astype(_bf16)
    y_ref[...] = x1 + _dot(act, wdown_ref[...])


def _merge_ffn(x, attn, d, ga, gb, w_pool, pool_scale, w_out, ln2_g, w_gate, w_up, w_down):
    N, D = x.shape
    tm = min(TOKEN_TILE, N)
    assert N % tm == 0 and tm % SUBLANES == 0
    d_ff = w_gate.shape[1]
    tok = pl.BlockSpec((tm, D), lambda t: (t, 0))
    return pl.pallas_call(
        _merge_ffn_kernel,
        out_shape=jax.ShapeDtypeStruct((N, D), _f32),
        grid=(N // tm,),
        in_specs=[tok, tok, tok, tok, tok,
                  _resident((N_POOL_GROUPS, POOL_GROUP_DIM, POOL_GROUP_DIM)), _resident((1, POOL_WIDTH)),
                  _resident((D, D)), _resident((1, D)),
                  _resident((D, d_ff)), _resident((D, d_ff)), _resident((d_ff, D))],
        out_specs=tok,
        compiler_params=pltpu.CompilerParams(
            dimension_semantics=("arbitrary",), vmem_limit_bytes=VMEM_LIMIT_BYTES),
        name="merge_ffn",
    )(x, attn, d, ga, gb, w_pool, pool_scale.reshape(1, POOL_WIDTH), w_out, ln2_g.reshape(1, D),
      w_gate, w_up, w_down)


def kernel(x_prompt, x_sample, cache_k, cache_v, state_pool, page_table, ln1_g, w_in, q_norm_g, k_norm_g,
           w_pool, pool_scale, w_out, ln2_g, w_gate, w_up, w_down):
    depth = w_in.shape[0]
    B, S, D = x_prompt.shape
    DB, DS, _ = x_sample.shape
    past_len = page_table.shape[1] * PAGE_SIZE
    x_p, x_s = x_prompt, x_sample
    kp_l, vp_l, pp_l, ks_l, vs_l, ps_l = [], [], [], [], [], []
    for l in range(depth):
        w_in_b, w_pool_b, w_out_b = w_in[l].astype(_bf16), w_pool[l].astype(_bf16), w_out[l].astype(_bf16)
        w_gate_b, w_up_b, w_down_b = w_gate[l].astype(_bf16), w_up[l].astype(_bf16), w_down[l].astype(_bf16)
        ffn = functools.partial(_merge_ffn, w_pool=w_pool_b, pool_scale=pool_scale[l], w_out=w_out_b,
                                ln2_g=ln2_g[l], w_gate=w_gate_b, w_up=w_up_b, w_down=w_down_b)

        q, k, v, d, ga, gb, hist = _proj(x_p, ln1_g[l], w_in_b, q_norm_g[l], k_norm_g[l], windows=True)
        attn = _attn_prompt(q, k, v)
        flat = lambda a: a.reshape(B * S, a.shape[-1])
        x_p = ffn(flat(x_p), flat(attn), flat(d), flat(ga), flat(gb)).reshape(B, S, D)
        kp_l.append(k)
        vp_l.append(v)
        pp_l.append(hist[:, HIST_ROWS - POOL_HIST:])

        q, k, v, u, ga, gb = _proj(x_s.reshape(1, DB * DS, D), ln1_g[l], w_in_b, q_norm_g[l], k_norm_g[l],
                                   windows=False)
        per_seq = lambda a: a.reshape(N_KV_HEADS, DB, DS, HEAD_DIM).transpose(1, 0, 2, 3)
        k, v = per_seq(k), per_seq(v)
        attn = _attn_sample(q.reshape(DB, DS, Q_COLS), k, v, cache_k, cache_v, page_table, l)
        u_ext = jnp.concatenate(
            [jnp.zeros((DB, HIST_ROWS - POOL_HIST, POOL_WIDTH), _f32), state_pool[l], u.reshape(DB, DS, POOL_WIDTH)],
            axis=1)
        d = _pool_windows(u_ext, DS, past_len)
        flat = lambda a: a.reshape(DB * DS, a.shape[-1])
        x_s = ffn(flat(x_s), flat(attn), flat(d), flat(ga), flat(gb)).reshape(DB, DS, D)
        ks_l.append(k)
        vs_l.append(v)
        ps_l.append(u_ext[:, HIST_ROWS + DS - POOL_HIST:])
    return (x_p, x_s, jnp.stack(kp_l), jnp.stack(vp_l), jnp.stack(pp_l),
            jnp.stack(ks_l), jnp.stack(vs_l), jnp.stack(ps_l))
```

```python
import functools

import jax
import jax.numpy as jnp
from jax import lax
from jax.experimental import pallas as pl
from jax.experimental.pallas import tpu as pltpu

D_MODEL = 1024
N_HEADS = 8
HEAD_DIM = D_MODEL // N_HEADS
N_KV_HEADS = 4
KV_GROUP = N_HEADS // N_KV_HEADS
MOBA_BLOCK = 256
MOBA_TOPK = 3
PAGE_SIZE = 128
ATTN_SCALE = HEAD_DIM ** -0.5
POOL_WINDOWS = (2, 4, 8, 16)
N_POOL_GROUPS = len(POOL_WINDOWS)
POOL_WIDTH = D_MODEL
POOL_GROUP_DIM = POOL_WIDTH // N_POOL_GROUPS
POOL_HIST = max(POOL_WINDOWS) - 1
RMS_EPS = 1e-6
Q_COLS = N_HEADS * HEAD_DIM
KV_COLS = N_KV_HEADS * HEAD_DIM
COL_Q = 0
COL_K = Q_COLS
COL_V = COL_K + KV_COLS
COL_U = COL_V + KV_COLS
COL_GA = COL_U + POOL_WIDTH
COL_GB = COL_GA + D_MODEL
IN_COLS = COL_GB + D_MODEL

SUBLANES = 8
LANES = 128
MXU_DIM = 256
VMEM_LIMIT_BYTES = 56 * 1024 * 1024

HIST_ROWS = 16
assert HIST_ROWS >= POOL_HIST and HIST_ROWS % SUBLANES == 0
MASK_NEG = -(2.0 ** 30)
LOG2_E = 1.4426950408889634

TOKEN_TILE = 256
ROW_CHUNK = 32
PAGES_PER_STEP = 16
BLOCKS_PER_STEP = PAGES_PER_STEP * PAGE_SIZE // MOBA_BLOCK
PAGE_SLOTS = 4
PREFETCH_JOBS = PAGE_SLOTS - 1

_f32 = jnp.float32
_bf16 = jnp.bfloat16


def _nt_dot(a, b, precision=None):
    return lax.dot_general(a, b, (((1,), (1,)), ((), ())), precision=precision,
                           preferred_element_type=_f32)


def _dot(a, b):
    return jnp.dot(a, b, preferred_element_type=_f32)


def _rms(x, g):
    return x * lax.rsqrt(jnp.mean(x * x, axis=-1, keepdims=True) + RMS_EPS) * g


def _sigmoid(x):
    return 1.0 / (1.0 + jnp.exp(-x))


def _resident(shape):
    nd = len(shape)
    return pl.BlockSpec(shape, lambda *_: (0,) * nd, pipeline_mode=pl.Buffered(1))


def _proj_kernel(x_ref, ln1_ref, w_ref, qg_ref, kg_ref,
                 q_ref, k_ref, v_ref, u_ref, ga_ref, gb_ref, *rest, tm, windows):
    if windows:
        hist_ref, carry_ref = rest
        si = pl.program_id(1)

        @pl.when(si == 0)
        def _():
            carry_ref[...] = jnp.zeros((HIST_ROWS, POOL_WIDTH), _f32)

    xn = _rms(x_ref[0], ln1_ref[...]).astype(_bf16)

    zu = _dot(xn, w_ref[:, COL_U:COL_GA])
    if windows:
        pos = si * tm + lax.broadcasted_iota(jnp.int32, (tm, 1), 0)
        for g, w in enumerate(POOL_WINDOWS):
            sl = slice(g * POOL_GROUP_DIM, (g + 1) * POOL_GROUP_DIM)
            run = jnp.concatenate([carry_ref[:, sl], zu[:, sl]], axis=0)
            half = 1
            while half < w:
                run = run + pltpu.roll(run, half, axis=0)
                half *= 2
            cnt = jnp.minimum(w, pos + 1).astype(_f32)
            u_ref[0, :, sl] = run[HIST_ROWS:] / cnt - zu[:, sl]
        hist_ref[0] = zu[tm - HIST_ROWS:tm, :]
        carry_ref[...] = zu[tm - HIST_ROWS:tm, :]
    else:
        u_ref[0] = zu

    ga_ref[0] = _sigmoid(_dot(xn, w_ref[:, COL_GA:COL_GB]))
    gb_ref[0] = _sigmoid(_dot(xn, w_ref[:, COL_GB:IN_COLS]))

    zq = _dot(xn, w_ref[:, COL_Q:COL_K])
    qg = qg_ref[...]
    for h in range(N_HEADS):
        sl = slice(h * HEAD_DIM, (h + 1) * HEAD_DIM)
        q_ref[0, :, sl] = _rms(zq[:, sl], qg)

    zk = _dot(xn, w_ref[:, COL_K:COL_V])
    zv = _dot(xn, w_ref[:, COL_V:COL_U])
    kg = kg_ref[...]
    for h in range(N_KV_HEADS):
        sl = slice(h * HEAD_DIM, (h + 1) * HEAD_DIM)
        k_ref[0, h] = _rms(zk[:, sl], kg)
        v_ref[0, h] = zv[:, sl]


def _proj(x, ln1_g, w_in_bf16, q_norm_g, k_norm_g, *, windows):
    B, S, D = x.shape
    tm = min(TOKEN_TILE, S)
    assert S % tm == 0 and tm % HIST_ROWS == 0
    tok = lambda n: pl.BlockSpec((1, tm, n), lambda b, s: (b, s, 0))
    kv = pl.BlockSpec((1, N_KV_HEADS, tm, HEAD_DIM), lambda b, s: (b, 0, s, 0))
    out_shape = [
        jax.ShapeDtypeStruct((B, S, Q_COLS), _f32),
        jax.ShapeDtypeStruct((B, N_KV_HEADS, S, HEAD_DIM), _f32),
        jax.ShapeDtypeStruct((B, N_KV_HEADS, S, HEAD_DIM), _f32),
        jax.ShapeDtypeStruct((B, S, POOL_WIDTH), _f32),
        jax.ShapeDtypeStruct((B, S, D_MODEL), _f32),
        jax.ShapeDtypeStruct((B, S, D_MODEL), _f32),
    ]
    out_specs = [tok(Q_COLS), kv, kv, tok(POOL_WIDTH), tok(D_MODEL), tok(D_MODEL)]
    scratch = []
    if windows:
        out_shape.append(jax.ShapeDtypeStruct((B, HIST_ROWS, POOL_WIDTH), _f32))
        out_specs.append(pl.BlockSpec((1, HIST_ROWS, POOL_WIDTH), lambda b, s: (b, 0, 0)))
        scratch.append(pltpu.VMEM((HIST_ROWS, POOL_WIDTH), _f32))
    return pl.pallas_call(
        functools.partial(_proj_kernel, tm=tm, windows=windows),
        out_shape=out_shape,
        grid=(B, S // tm),
        in_specs=[tok(D), _resident((1, D)), _resident((D, IN_COLS)),
                  _resident((1, HEAD_DIM)), _resident((1, HEAD_DIM))],
        out_specs=out_specs,
        scratch_shapes=scratch,
        compiler_params=pltpu.CompilerParams(
            dimension_semantics=("arbitrary", "arbitrary"), vmem_limit_bytes=VMEM_LIMIT_BYTES),
        name="proj_windows" if windows else "proj",
    )(x, ln1_g.reshape(1, D), w_in_bf16, q_norm_g.reshape(1, HEAD_DIM), k_norm_g.reshape(1, HEAD_DIM))


def _topk_select(g, valid, row_idx, n_rows, axis):
    gm = jnp.where(valid, g, -jnp.inf)
    cnt = jnp.zeros(g.shape, jnp.int32)
    for jp in range(n_rows):
        gj = lax.slice_in_dim(gm, jp, jp + 1, axis=axis)
        beats = (gj > gm) | ((gj == gm) & (jp < row_idx))
        cnt = cnt + jnp.where(beats, 1, 0)
    return valid & (cnt < MOBA_TOPK)


def _split_bf16(x):
    hi = x.astype(_bf16)
    return hi, (x - hi.astype(_f32)).astype(_bf16)


N_PARTS = 3


def _split3(x):
    parts = []
    for _ in range(N_PARTS):
        part = x.astype(_bf16).astype(_f32)
        parts.append(part)
        x = x - part
    return parts


def _attn_prompt_kernel(q_ref, k_ref, v_ref, slope_ref, o_ref, kaug_ref, vt_ref, s_ref, p_ref, *, nb, nbp):
    S = nb * MOBA_BLOCK
    QR = KV_GROUP * MOBA_BLOCK
    KM_ROWS = 2 * SUBLANES

    k = k_ref[0, 0]
    kaug_ref[:, 0:HEAD_DIM] = k.astype(_bf16)
    row = lax.broadcasted_iota(jnp.int32, (S, LANES), 0)
    lane = lax.broadcasted_iota(jnp.int32, (S, LANES), 1)
    onehot = jnp.where((lane % nbp == row // MOBA_BLOCK) & (lane < N_PARTS * nbp), 1.0, 0.0)
    is_rpos = (lane >= N_PARTS * nbp) & (lane < N_PARTS * (nbp + SUBLANES)) & (lane % SUBLANES == 0)
    rpos = jnp.where(is_rpos, (row % MOBA_BLOCK).astype(_f32), 0.0)
    kaug_ref[:, HEAD_DIM:HEAD_DIM + LANES] = (onehot + rpos).astype(_bf16)
    vt_ref[...] = v_ref[0, 0].T.astype(_bf16)

    kmean = jnp.mean(k.reshape(nb, MOBA_BLOCK, HEAD_DIM), axis=1)
    km = jnp.concatenate([kmean, jnp.zeros((KM_ROWS - nb, HEAD_DIM), _f32)], axis=0)
    km_hi, km_lo = _split_bf16(km)
    km_cat = jnp.concatenate([km_hi, km_hi, km_lo], axis=1)

    slope_rows = slope_ref[0]
    slope = slope_rows[0:1, :]
    jidx = lax.broadcasted_iota(jnp.int32, (nbp, QR), 0)
    key_r = lax.broadcasted_iota(jnp.int32, (MOBA_BLOCK, QR), 0)
    t_rel = lax.broadcasted_iota(jnp.int32, (MOBA_BLOCK, QR), 1) % MOBA_BLOCK
    causal = key_r <= t_rel

    def augmented_queries(i):
        rows = slice(i * MOBA_BLOCK, (i + 1) * MOBA_BLOCK)
        q2 = q_ref[0, rows, :]
        q_t = jnp.concatenate([q2[:, h * HEAD_DIM:(h + 1) * HEAD_DIM].T for h in range(KV_GROUP)], axis=1)

        if i > MOBA_TOPK:
            q_hi, q_lo = _split_bf16(q_t)
            gate_t = _dot(km_cat, jnp.concatenate([q_hi, q_lo, q_hi], axis=0))[0:nbp]
            sel = _topk_select(gate_t, jidx < i, jidx, i, axis=0)
        else:
            sel = jidx < i
        bias_t = jnp.where(sel, slope * (MOBA_BLOCK * (jidx - i)).astype(_f32), MASK_NEG)
        bias_t = jnp.where(jidx == i, 0.0, bias_t)
        aux_t = jnp.concatenate(
            _split3(bias_t * LOG2_E) + _split3(slope_rows * LOG2_E)
            + [jnp.zeros((LANES - N_PARTS * (nbp + SUBLANES), QR), _f32)], axis=0)
        return jnp.concatenate([(q_t * (ATTN_SCALE * LOG2_E)).astype(_bf16), aux_t.astype(_bf16)], axis=0)

    def scores(i):
        qa_t = augmented_queries(i)
        mt = None
        for j in range(i + 1):
            blk = slice(j * MOBA_BLOCK, (j + 1) * MOBA_BLOCK)
            sj = _dot(kaug_ref[blk, :], qa_t)
            if j == i:
                sj = jnp.where(causal, sj, MASK_NEG)
            s_ref[i % 2, blk, :] = sj
            for r in range(0, MOBA_BLOCK, ROW_CHUNK):
                mt = sj[r:r + ROW_CHUNK] if mt is None else jnp.maximum(mt, sj[r:r + ROW_CHUNK])
        return jnp.max(mt, axis=0, keepdims=True)

    def finish(i, m):
        n = (i + 1) * MOBA_BLOCK
        lt = jnp.zeros((ROW_CHUNK, QR), _f32)
        for r in range(0, n, ROW_CHUNK):
            p = jnp.exp2(s_ref[i % 2, r:r + ROW_CHUNK, :] - m)
            lt = lt + p
            p_ref[i % 2, r:r + ROW_CHUNK, :] = p.astype(_bf16)
        l = jnp.sum(lt, axis=0, keepdims=True)
        o_t = _dot(vt_ref[:, 0:n], p_ref[i % 2, 0:n, :]) / l
        o_ref[0, i * MOBA_BLOCK:n, :] = jnp.concatenate(
            [o_t[:, h * MOBA_BLOCK:(h + 1) * MOBA_BLOCK].T for h in range(KV_GROUP)], axis=1)

    m_next = scores(0)
    for i in range(nb):
        m_cur = m_next
        if i + 1 < nb:
            m_next = scores(i + 1)
        finish(i, m_cur)


def _slope_table(rows_per_head):
    slopes = 2.0 ** (-8.0 * jnp.arange(1, N_HEADS + 1, dtype=_f32) / N_HEADS)
    per_row = jnp.repeat(slopes.reshape(N_KV_HEADS, KV_GROUP), rows_per_head, axis=1)
    tab = jnp.zeros((N_KV_HEADS, SUBLANES, KV_GROUP * rows_per_head), _f32)
    return tab.at[:, 0, :].set(per_row)


def _attn_prompt(q, k, v):
    B, S, _ = q.shape
    assert S % MOBA_BLOCK == 0
    nb = S // MOBA_BLOCK
    nbp = -(-nb // SUBLANES) * SUBLANES
    assert N_PARTS * (nbp + SUBLANES) <= LANES
    gw = KV_GROUP * HEAD_DIM
    QR = KV_GROUP * MOBA_BLOCK
    assert nb <= 2 * SUBLANES
    qo_spec = pl.BlockSpec((1, S, gw), lambda b, h: (b, 0, h))
    kv_spec = pl.BlockSpec((1, 1, S, HEAD_DIM), lambda b, h: (b, h, 0, 0))
    return pl.pallas_call(
        functools.partial(_attn_prompt_kernel, nb=nb, nbp=nbp),
        out_shape=jax.ShapeDtypeStruct((B, S, Q_COLS), _f32),
        grid=(B, N_KV_HEADS),
        in_specs=[qo_spec, kv_spec, kv_spec,
                  pl.BlockSpec((1, SUBLANES, QR), lambda b, h: (h, 0, 0))],
        out_specs=qo_spec,
        scratch_shapes=[pltpu.VMEM((S, HEAD_DIM + LANES), _bf16),
                        pltpu.VMEM((HEAD_DIM, S), _bf16),
                        pltpu.VMEM((2, S, QR), _f32),
                        pltpu.VMEM((2, S, QR), _bf16)],
        compiler_params=pltpu.CompilerParams(
            dimension_semantics=("arbitrary", "arbitrary"),
            vmem_limit_bytes=VMEM_LIMIT_BYTES),
        name="attn_prompt",
    )(q, k, v, _slope_table(MOBA_BLOCK))


def _attn_sample_kernel(pt_ref, q_ref, kn_ref, vn_ref, slope_row_ref, slope_col_ref, expand_ref, ck_hbm, cv_hbm,
                        o_ref, pbuf, sem, s_ref, p_ref, ksum_ref, *, layer, nc, ds, nblk, n_seq):
    P = PAGES_PER_STEP
    b = pl.program_id(0)
    njobs = 2 * nc
    QR = KV_GROUP * ds
    R = N_KV_HEADS * QR

    def chunk_copies(seq, job):
        src = ck_hbm if job < nc else cv_hbm
        slot = job % PAGE_SLOTS
        return [pltpu.make_async_copy(src.at[layer, pt_ref[seq, (job % nc) * P + pg]], pbuf.at[slot, pg], sem.at[slot])
                for pg in range(P)]

    def start(seq, job):
        for cp in chunk_copies(seq, job):
            cp.start()

    @pl.when(b == 0)
    def _():
        for job in range(PREFETCH_JOBS):
            start(b, job)

    def begin_job(job):
        for cp in chunk_copies(b, job):
            cp.wait()
        ahead = job + PREFETCH_JOBS
        if ahead < njobs:
            start(b, ahead)
        else:
            @pl.when(b + 1 < n_seq)
            def _():
                start(b + 1, ahead - njobs)
        return job % PAGE_SLOTS

    def stacked_q(h):
        q = q_ref[0]
        return jnp.concatenate(
            [q[:, (KV_GROUP * h + g) * HEAD_DIM:(KV_GROUP * h + g + 1) * HEAD_DIM] for g in range(KV_GROUP)],
            axis=0)

    ksum_ref[...] = jnp.zeros(ksum_ref.shape, _f32)
    for c in range(nc):
        slot = begin_job(c)
        for h in range(N_KV_HEADS):
            qs = (stacked_q(h) * ATTN_SCALE).astype(_bf16)
            sums = []
            for pg in range(P):
                kp = pbuf[slot, pg, h]
                s_ref[c, h * QR:(h + 1) * QR, pg * PAGE_SIZE:(pg + 1) * PAGE_SIZE] = _nt_dot(qs, kp.astype(_bf16))
                sums.append(jnp.sum(kp, axis=0, keepdims=True))
            ppb = MOBA_BLOCK // PAGE_SIZE
            bs = [functools.reduce(lambda a, b: a + b, sums[j * ppb:(j + 1) * ppb]) for j in range(P // ppb)]
            ksum_ref[h, c * BLOCKS_PER_STEP:(c + 1) * BLOCKS_PER_STEP, :] = jnp.concatenate(bs, axis=0)

    qs = [stacked_q(h) for h in range(N_KV_HEADS)]
    gate = jnp.concatenate(
        [_nt_dot(qs[h], ksum_ref[h] / MOBA_BLOCK, precision=lax.Precision.HIGHEST) for h in range(N_KV_HEADS)]
        + [jnp.zeros((LANES - R, LANES), _f32)], axis=0)
    g_t = gate.T[0:nblk]
    jidx = lax.broadcasted_iota(jnp.int32, (nblk, LANES), 0)
    sel = _topk_select(g_t, jidx < nblk, jidx, nblk, axis=0)
    slope_rows = slope_row_ref[...]
    bias_t = jnp.where(sel, slope_rows[0:1, :] * (MOBA_BLOCK * (jidx - nblk)).astype(_f32), MASK_NEG)
    aux_t = jnp.concatenate(
        [bias_t, slope_rows, jnp.zeros((LANES - nblk - SUBLANES, LANES), _f32)], axis=0)
    aux = aux_t.T[0:R].astype(_bf16)

    m = jnp.full((R, 1), MASK_NEG, _f32)
    for cc in range(nc):
        sc = s_ref[cc] + _dot(aux, expand_ref[cc])
        s_ref[cc] = sc
        m = jnp.maximum(m, jnp.max(sc, axis=-1, keepdims=True))
    so = jnp.concatenate(
        [_nt_dot((qs[h] * ATTN_SCALE).astype(_bf16), kn_ref[0, h].astype(_bf16)) for h in range(N_KV_HEADS)],
        axis=0)
    tt = lax.broadcasted_iota(jnp.int32, (R, ds), 0) % ds
    ss = lax.broadcasted_iota(jnp.int32, (R, ds), 1)
    so = jnp.where(ss <= tt, so + slope_col_ref[:, 0:1] * ss.astype(_f32), MASK_NEG)
    m = jnp.maximum(m, jnp.max(so, axis=-1, keepdims=True))
    po = jnp.exp(so - m)
    l = jnp.sum(po, axis=-1, keepdims=True)
    for cc in range(nc):
        p = jnp.exp(s_ref[cc] - m)
        l = l + jnp.sum(p, axis=-1, keepdims=True)
        p_ref[cc] = p.astype(_bf16)
    acc = [_dot(po[h * QR:(h + 1) * QR].astype(_bf16), vn_ref[0, h].astype(_bf16)) for h in range(N_KV_HEADS)]

    for c in range(nc):
        slot = begin_job(nc + c)
        p = p_ref[c]
        for h in range(N_KV_HEADS):
            for pg in range(P):
                vp = pbuf[slot, pg, h].astype(_bf16)
                acc[h] = acc[h] + _dot(p[h * QR:(h + 1) * QR, pg * PAGE_SIZE:(pg + 1) * PAGE_SIZE], vp)

    for h in range(N_KV_HEADS):
        o = acc[h] / l[h * QR:(h + 1) * QR]
        for g in range(KV_GROUP):
            hh = KV_GROUP * h + g
            o_ref[0, :, hh * HEAD_DIM:(hh + 1) * HEAD_DIM] = o[g * ds:(g + 1) * ds]


def _attn_sample(q, k_new, v_new, cache_k, cache_v, page_table, layer):
    DB, DS, _ = q.shape
    n_pages = page_table.shape[1]
    past = n_pages * PAGE_SIZE
    assert past % MOBA_BLOCK == 0 and DS <= MOBA_BLOCK and DS % SUBLANES == 0
    nblk = past // MOBA_BLOCK
    assert nblk >= MOBA_TOPK and nblk % SUBLANES == 0 and nblk + SUBLANES <= LANES
    P = PAGES_PER_STEP
    assert n_pages % P == 0 and BLOCKS_PER_STEP % SUBLANES == 0
    nc = n_pages // P
    QR = KV_GROUP * DS
    R = N_KV_HEADS * QR
    assert R <= LANES and QR % (2 * SUBLANES) == 0
    CW = P * PAGE_SIZE

    slopes = 2.0 ** (-8.0 * jnp.arange(1, N_HEADS + 1, dtype=_f32) / N_HEADS)
    per_row = jnp.repeat(slopes, DS)
    slope_row = jnp.zeros((SUBLANES, LANES), _f32).at[0, :R].set(per_row)
    slope_col = jnp.broadcast_to(per_row[:, None], (R, LANES))
    key = jnp.arange(past)
    feat = jnp.arange(LANES)[:, None]
    expand = jnp.where(feat == nblk, (key % MOBA_BLOCK)[None, :].astype(_f32),
                       (feat == (key // MOBA_BLOCK)[None, :]).astype(_f32))
    expand = expand.reshape(LANES, nc, CW).transpose(1, 0, 2).astype(_bf16)

    assert (2 * nc) % PAGE_SLOTS == 0 and PREFETCH_JOBS == PAGE_SLOTS - 1 and PREFETCH_JOBS <= 2 * nc

    tok_spec = pl.BlockSpec((1, DS, Q_COLS), lambda b, pt: (b, 0, 0))
    new_spec = pl.BlockSpec((1, N_KV_HEADS, DS, HEAD_DIM), lambda b, pt: (b, 0, 0, 0))
    hbm = pl.BlockSpec(memory_space=pl.ANY)
    in_specs = [tok_spec, new_spec, new_spec,
                pl.BlockSpec((SUBLANES, LANES), lambda b, pt: (0, 0)),
                pl.BlockSpec((R, LANES), lambda b, pt: (0, 0)),
                pl.BlockSpec((nc, LANES, CW), lambda b, pt: (0, 0, 0)),
                hbm, hbm]

    return pl.pallas_call(
        functools.partial(_attn_sample_kernel, layer=layer, nc=nc, ds=DS, nblk=nblk, n_seq=DB),
        out_shape=jax.ShapeDtypeStruct((DB, DS, Q_COLS), _f32),
        grid_spec=pltpu.PrefetchScalarGridSpec(
            num_scalar_prefetch=1,
            grid=(DB,),
            in_specs=in_specs,
            out_specs=tok_spec,
            scratch_shapes=[pltpu.VMEM((PAGE_SLOTS, P, N_KV_HEADS, PAGE_SIZE, HEAD_DIM), _f32),
                            pltpu.SemaphoreType.DMA((PAGE_SLOTS,)),
                            pltpu.VMEM((nc, R, CW), _f32),
                            pltpu.VMEM((nc, R, CW), _bf16),
                            pltpu.VMEM((N_KV_HEADS, LANES, HEAD_DIM), _f32)]),
        compiler_params=pltpu.CompilerParams(
            dimension_semantics=("arbitrary",), vmem_limit_bytes=VMEM_LIMIT_BYTES),
        name="attn_sample",
    )(page_table, q, k_new, v_new, slope_row, slope_col, expand, cache_k, cache_v)


def _pool_windows_kernel(uext_ref, d_ref, *, ds, first_pos):
    cur = uext_ref[:, HIST_ROWS:HIST_ROWS + ds, :]
    pos = first_pos + lax.broadcasted_iota(jnp.int32, (1, ds, 1), 1)
    for g, w in enumerate(POOL_WINDOWS):
        sl = slice(g * POOL_GROUP_DIM, (g + 1) * POOL_GROUP_DIM)
        win = cur[:, :, sl]
        for dd in range(1, w):
            win = win + uext_ref[:, HIST_ROWS - dd:HIST_ROWS - dd + ds, sl]
        cnt = jnp.minimum(w, pos + 1).astype(_f32)
        d_ref[:, :, sl] = win / cnt - cur[:, :, sl]


def _pool_windows(u_ext_padded, ds, first_pos):
    DB, R, P = u_ext_padded.shape
    assert R == HIST_ROWS + ds
    bb = 16
    assert DB % bb == 0
    return pl.pallas_call(
        functools.partial(_pool_windows_kernel, ds=ds, first_pos=first_pos),
        out_shape=jax.ShapeDtypeStruct((DB, ds, P), _f32),
        grid=(DB // bb,),
        in_specs=[pl.BlockSpec((bb, R, P), lambda b: (b, 0, 0))],
        out_specs=pl.BlockSpec((bb, ds, P), lambda b: (b, 0, 0)),
        compiler_params=pltpu.CompilerParams(dimension_semantics=("arbitrary",)),
        name="pool_windows",
    )(u_ext_padded)


def _merge_ffn_kernel(x_ref, attn_ref, d_ref, sga_ref, sgb_ref, wpool_ref, pscale_ref, wout_ref, ln2_ref,
                      wgate_ref, wup_ref, wdown_ref, y_ref):
    d = d_ref[...]
    pool = jnp.concatenate(
        [_dot(d[:, g * POOL_GROUP_DIM:(g + 1) * POOL_GROUP_DIM].astype(_bf16), wpool_ref[g])
         for g in range(N_POOL_GROUPS)], axis=1) * pscale_ref[...]
    h = sga_ref[...] * attn_ref[...] + sgb_ref[...] * pool
    x1 = x_ref[...] + _dot(h.astype(_bf16), wout_ref[...])
    hn = _rms(x1, ln2_ref[...]).astype(_bf16)
    gate = _dot(hn, wgate_ref[...])
    up = _dot(hn, wup_ref[...])
    act = (gate * _sigmoid(gate) * up).astype(_bf16)
    y_ref[...] = x1 + _dot(act, wdown_ref[...])


def _merge_ffn(x, attn, d, ga, gb, w_pool, pool_scale, w_out, ln2_g, w_gate, w_up, w_down):
    N, D = x.shape
    tm = min(TOKEN_TILE, N)
    assert N % tm == 0 and tm % SUBLANES == 0
    d_ff = w_gate.shape[1]
    tok = pl.BlockSpec((tm, D), lambda t: (t, 0))
    return pl.pallas_call(
        _merge_ffn_kernel,
        out_shape=jax.ShapeDtypeStruct((N, D), _f32),
        grid=(N // tm,),
        in_specs=[tok, tok, tok, tok, tok,
                  _resident((N_POOL_GROUPS, POOL_GROUP_DIM, POOL_GROUP_DIM)), _resident((1, POOL_WIDTH)),
                  _resident((D, D)), _resident((1, D)),
                  _resident((D, d_ff)), _resident((D, d_ff)), _resident((d_ff, D))],
        out_specs=tok,
        compiler_params=pltpu.CompilerParams(
            dimension_semantics=("arbitrary",), vmem_limit_bytes=VMEM_LIMIT_BYTES),
        name="merge_ffn",
    )(x, attn, d, ga, gb, w_pool, pool_scale.reshape(1, POOL_WIDTH), w_out, ln2_g.reshape(1, D),
      w_gate, w_up, w_down)


def kernel(x_prompt, x_sample, cache_k, cache_v, state_pool, page_table, ln1_g, w_in, q_norm_g, k_norm_g,
           w_pool, pool_scale, w_out, ln2_g, w_gate, w_up, w_down):
    depth = w_in.shape[0]
    B, S, D = x_prompt.shape
    DB, DS, _ = x_sample.shape
    past_len = page_table.shape[1] * PAGE_SIZE
    x_p, x_s = x_prompt, x_sample
    kp_l, vp_l, pp_l, ks_l, vs_l, ps_l = [], [], [], [], [], []
    for l in range(depth):
        w_in_b, w_pool_b, w_out_b = w_in[l].astype(_bf16), w_pool[l].astype(_bf16), w_out[l].astype(_bf16)
        w_gate_b, w_up_b, w_down_b = w_gate[l].astype(_bf16), w_up[l].astype(_bf16), w_down[l].astype(_bf16)
        ffn = functools.partial(_merge_ffn, w_pool=w_pool_b, pool_scale=pool_scale[l], w_out=w_out_b,
                                ln2_g=ln2_g[l], w_gate=w_gate_b, w_up=w_up_b, w_down=w_down_b)

        q, k, v, d, ga, gb, hist = _proj(x_p, ln1_g[l], w_in_b, q_norm_g[l], k_norm_g[l], windows=True)
        attn = _attn_prompt(q, k, v)
        flat = lambda a: a.reshape(B * S, a.shape[-1])
        x_p = ffn(flat(x_p), flat(attn), flat(d), flat(ga), flat(gb)).reshape(B, S, D)
        kp_l.append(k)
        vp_l.append(v)
        pp_l.append(hist[:, HIST_ROWS - POOL_HIST:])

        q, k, v, u, ga, gb = _proj(x_s.reshape(1, DB * DS, D), ln1_g[l], w_in_b, q_norm_g[l], k_norm_g[l],
                                   windows=False)
        per_seq = lambda a: a.reshape(N_KV_HEADS, DB, DS, HEAD_DIM).transpose(1, 0, 2, 3)
        k, v = per_seq(k), per_seq(v)
        attn = _attn_sample(q.reshape(DB, DS, Q_COLS), k, v, cache_k, cache_v, page_table, l)
        u_ext = jnp.concatenate(
            [jnp.zeros((DB, HIST_ROWS - POOL_HIST, POOL_WIDTH), _f32), state_pool[l], u.reshape(DB, DS, POOL_WIDTH)],
            axis=1)
        d = _pool_windows(u_ext, DS, past_len)
        flat = lambda a: a.reshape(DB * DS, a.shape[-1])
        x_s = ffn(flat(x_s), flat(attn), flat(d), flat(ga), flat(gb)).reshape(DB, DS, D)
        ks_l.append(k)
        vs_l.append(v)
        ps_l.append(u_ext[:, HIST_ROWS + DS - POOL_HIST:])
    return (x_p, x_s, jnp.stack(kp_l), jnp.stack(vp_l), jnp.stack(pp_l),
            jnp.stack(ks_l), jnp.stack(vs_l), jnp.stack(ps_l))
```

```python
import functools

import jax
import jax.numpy as jnp
from jax import lax
from jax.experimental import pallas as pl
from jax.experimental.pallas import tpu as pltpu

D_MODEL = 1024
N_HEADS = 8
HEAD_DIM = D_MODEL // N_HEADS
N_KV_HEADS = 4
KV_GROUP = N_HEADS // N_KV_HEADS
MOBA_BLOCK = 256
MOBA_TOPK = 3
PAGE_SIZE = 128
ATTN_SCALE = HEAD_DIM ** -0.5
POOL_WINDOWS = (2, 4, 8, 16)
N_POOL_GROUPS = len(POOL_WINDOWS)
POOL_WIDTH = D_MODEL
POOL_GROUP_DIM = POOL_WIDTH // N_POOL_GROUPS
POOL_HIST = max(POOL_WINDOWS) - 1
RMS_EPS = 1e-6
Q_COLS = N_HEADS * HEAD_DIM
KV_COLS = N_KV_HEADS * HEAD_DIM
COL_Q = 0
COL_K = Q_COLS
COL_V = COL_K + KV_COLS
COL_U = COL_V + KV_COLS
COL_GA = COL_U + POOL_WIDTH
COL_GB = COL_GA + D_MODEL
IN_COLS = COL_GB + D_MODEL

SUBLANES = 8
LANES = 128
MXU_DIM = 256
VMEM_LIMIT_BYTES = 56 * 1024 * 1024

HIST_ROWS = 16
assert HIST_ROWS >= POOL_HIST and HIST_ROWS % SUBLANES == 0
MASK_NEG = -(2.0 ** 30)
LOG2_E = 1.4426950408889634

TOKEN_TILE = 256
ROW_CHUNK = 32
SCORE_SLOTS = 3
PAGES_PER_STEP = 16
BLOCKS_PER_STEP = PAGES_PER_STEP * PAGE_SIZE // MOBA_BLOCK
PAGE_SLOTS = 4
PREFETCH_JOBS = PAGE_SLOTS - 1

_f32 = jnp.float32
_bf16 = jnp.bfloat16


def _nt_dot(a, b, precision=None):
    return lax.dot_general(a, b, (((1,), (1,)), ((), ())), precision=precision,
                           preferred_element_type=_f32)


def _dot(a, b):
    return jnp.dot(a, b, preferred_element_type=_f32)


def _rms(x, g):
    return x * lax.rsqrt(jnp.mean(x * x, axis=-1, keepdims=True) + RMS_EPS) * g


def _sigmoid(x):
    return 1.0 / (1.0 + jnp.exp(-x))


def _resident(shape):
    nd = len(shape)
    return pl.BlockSpec(shape, lambda *_: (0,) * nd, pipeline_mode=pl.Buffered(1))


def _proj_kernel(x_ref, ln1_ref, w_ref, qg_ref, kg_ref,
                 q_ref, k_ref, v_ref, u_ref, ga_ref, gb_ref, *rest, tm, windows):
    if windows:
        hist_ref, carry_ref = rest
        si = pl.program_id(1)

        @pl.when(si == 0)
        def _():
            carry_ref[...] = jnp.zeros((HIST_ROWS, POOL_WIDTH), _f32)

    xn = _rms(x_ref[0], ln1_ref[...]).astype(_bf16)

    zu = _dot(xn, w_ref[:, COL_U:COL_GA])
    ga_ref[0] = _sigmoid(_dot(xn, w_ref[:, COL_GA:COL_GB]))
    gb_ref[0] = _sigmoid(_dot(xn, w_ref[:, COL_GB:IN_COLS]))
    if windows:
        pos = si * tm + lax.broadcasted_iota(jnp.int32, (tm, 1), 0)
        for g, w in enumerate(POOL_WINDOWS):
            sl = slice(g * POOL_GROUP_DIM, (g + 1) * POOL_GROUP_DIM)
            run = jnp.concatenate([carry_ref[:, sl], zu[:, sl]], axis=0)
            half = 1
            while half < w:
                run = run + pltpu.roll(run, half, axis=0)
                half *= 2
            cnt = jnp.minimum(w, pos + 1).astype(_f32)
            u_ref[0, :, sl] = run[HIST_ROWS:] / cnt - zu[:, sl]
        hist_ref[0] = zu[tm - HIST_ROWS:tm, :]
        carry_ref[...] = zu[tm - HIST_ROWS:tm, :]
    else:
        u_ref[0] = zu

    zq = _dot(xn, w_ref[:, COL_Q:COL_K])
    qg = qg_ref[...]
    for h in range(N_HEADS):
        sl = slice(h * HEAD_DIM, (h + 1) * HEAD_DIM)
        q_ref[0, :, sl] = _rms(zq[:, sl], qg)

    zk = _dot(xn, w_ref[:, COL_K:COL_V])
    zv = _dot(xn, w_ref[:, COL_V:COL_U])
    kg = kg_ref[...]
    for h in range(N_KV_HEADS):
        sl = slice(h * HEAD_DIM, (h + 1) * HEAD_DIM)
        k_ref[0, h] = _rms(zk[:, sl], kg)
        v_ref[0, h] = zv[:, sl]


def _proj(x, ln1_g, w_in_bf16, q_norm_g, k_norm_g, *, windows):
    B, S, D = x.shape
    tm = min(TOKEN_TILE, S)
    assert S % tm == 0 and tm % HIST_ROWS == 0
    tok = lambda n: pl.BlockSpec((1, tm, n), lambda b, s: (b, s, 0))
    kv = pl.BlockSpec((1, N_KV_HEADS, tm, HEAD_DIM), lambda b, s: (b, 0, s, 0))
    out_shape = [
        jax.ShapeDtypeStruct((B, S, Q_COLS), _f32),
        jax.ShapeDtypeStruct((B, N_KV_HEADS, S, HEAD_DIM), _f32),
        jax.ShapeDtypeStruct((B, N_KV_HEADS, S, HEAD_DIM), _f32),
        jax.ShapeDtypeStruct((B, S, POOL_WIDTH), _f32),
        jax.ShapeDtypeStruct((B, S, D_MODEL), _f32),
        jax.ShapeDtypeStruct((B, S, D_MODEL), _f32),
    ]
    out_specs = [tok(Q_COLS), kv, kv, tok(POOL_WIDTH), tok(D_MODEL), tok(D_MODEL)]
    scratch = []
    if windows:
        out_shape.append(jax.ShapeDtypeStruct((B, HIST_ROWS, POOL_WIDTH), _f32))
        out_specs.append(pl.BlockSpec((1, HIST_ROWS, POOL_WIDTH), lambda b, s: (b, 0, 0)))
        scratch.append(pltpu.VMEM((HIST_ROWS, POOL_WIDTH), _f32))
    return pl.pallas_call(
        functools.partial(_proj_kernel, tm=tm, windows=windows),
        out_shape=out_shape,
        grid=(B, S // tm),
        in_specs=[tok(D), _resident((1, D)), _resident((D, IN_COLS)),
                  _resident((1, HEAD_DIM)), _resident((1, HEAD_DIM))],
        out_specs=out_specs,
        scratch_shapes=scratch,
        compiler_params=pltpu.CompilerParams(
            dimension_semantics=("arbitrary", "arbitrary"), vmem_limit_bytes=VMEM_LIMIT_BYTES),
        name="proj_windows" if windows else "proj",
    )(x, ln1_g.reshape(1, D), w_in_bf16, q_norm_g.reshape(1, HEAD_DIM), k_norm_g.reshape(1, HEAD_DIM))


def _topk_select(g, valid, row_idx, n_rows, axis):
    gm = jnp.where(valid, g, -jnp.inf)
    cnt = jnp.zeros(g.shape, jnp.int32)
    for jp in range(n_rows):
        gj = lax.slice_in_dim(gm, jp, jp + 1, axis=axis)
        beats = (gj > gm) | ((gj == gm) & (jp < row_idx))
        cnt = cnt + jnp.where(beats, 1, 0)
    return valid & (cnt < MOBA_TOPK)


def _split_bf16(x):
    hi = x.astype(_bf16)
    return hi, (x - hi.astype(_f32)).astype(_bf16)


N_PARTS = 3


def _split3(x):
    parts = []
    for _ in range(N_PARTS):
        part = x.astype(_bf16).astype(_f32)
        parts.append(part)
        x = x - part
    return parts


def _prompt_program(q_ref, k_ref, v_ref, slope_ref, o_ref, kaug_ref, vt_ref, s_ref, p_ref, *, nb, nbp):
    S = nb * MOBA_BLOCK
    QR = KV_GROUP * MOBA_BLOCK
    KM_ROWS = 2 * SUBLANES

    k = k_ref[0, 0]
    kaug_ref[:, 0:HEAD_DIM] = k.astype(_bf16)
    row = lax.broadcasted_iota(jnp.int32, (S, LANES), 0)
    lane = lax.broadcasted_iota(jnp.int32, (S, LANES), 1)
    onehot = jnp.where((lane % nbp == row // MOBA_BLOCK) & (lane < N_PARTS * nbp), 1.0, 0.0)
    is_rpos = (lane >= N_PARTS * nbp) & (lane < N_PARTS * (nbp + SUBLANES)) & (lane % SUBLANES == 0)
    rpos = jnp.where(is_rpos, (row % MOBA_BLOCK).astype(_f32), 0.0)
    kaug_ref[:, HEAD_DIM:HEAD_DIM + LANES] = (onehot + rpos).astype(_bf16)
    vt_ref[...] = v_ref[0, 0].T.astype(_bf16)

    kmean = jnp.mean(k.reshape(nb, MOBA_BLOCK, HEAD_DIM), axis=1)
    km = jnp.concatenate([kmean, jnp.zeros((KM_ROWS - nb, HEAD_DIM), _f32)], axis=0)
    km_hi, km_lo = _split_bf16(km)
    km_cat = jnp.concatenate([km_hi, km_hi, km_lo], axis=1)

    slope_rows = slope_ref[0]
    slope = slope_rows[0:1, :]
    jidx = lax.broadcasted_iota(jnp.int32, (nbp, QR), 0)
    key_r = lax.broadcasted_iota(jnp.int32, (MOBA_BLOCK, QR), 0)
    t_rel = lax.broadcasted_iota(jnp.int32, (MOBA_BLOCK, QR), 1) % MOBA_BLOCK
    causal = key_r <= t_rel

    def augmented_queries(i):
        rows = slice(i * MOBA_BLOCK, (i + 1) * MOBA_BLOCK)
        q2 = q_ref[0, rows, :]
        q_t = jnp.concatenate([q2[:, h * HEAD_DIM:(h + 1) * HEAD_DIM].T for h in range(KV_GROUP)], axis=1)

        if i > MOBA_TOPK:
            q_hi, q_lo = _split_bf16(q_t)
            gate_t = _dot(km_cat, jnp.concatenate([q_hi, q_lo, q_hi], axis=0))[0:nbp]
            sel = _topk_select(gate_t, jidx < i, jidx, i, axis=0)
        else:
            sel = jidx < i
        bias_t = jnp.where(sel, slope * (MOBA_BLOCK * (jidx - i)).astype(_f32), MASK_NEG)
        bias_t = jnp.where(jidx == i, 0.0, bias_t)
        aux_t = jnp.concatenate(
            _split3(bias_t * LOG2_E) + _split3(slope_rows * LOG2_E)
            + [jnp.zeros((LANES - N_PARTS * (nbp + SUBLANES), QR), _f32)], axis=0)
        return jnp.concatenate([(q_t * (ATTN_SCALE * LOG2_E)).astype(_bf16), aux_t.astype(_bf16)], axis=0)

    row_max = {}

    def scores(i):
        qa_t = augmented_queries(i)
        mt = None
        for j in range(i + 1):
            blk = slice(j * MOBA_BLOCK, (j + 1) * MOBA_BLOCK)
            sj = _dot(kaug_ref[blk, :], qa_t)
            if j == i:
                sj = jnp.where(causal, sj, MASK_NEG)
            s_ref[i % SCORE_SLOTS, blk, :] = sj
            for r in range(0, MOBA_BLOCK, ROW_CHUNK):
                mt = sj[r:r + ROW_CHUNK] if mt is None else jnp.maximum(mt, sj[r:r + ROW_CHUNK])
        row_max[i] = jnp.max(mt, axis=0, keepdims=True)

    def finish(i):
        m = row_max.pop(i)
        n = (i + 1) * MOBA_BLOCK
        lt = jnp.zeros((ROW_CHUNK, QR), _f32)
        for r in range(0, n, ROW_CHUNK):
            p = jnp.exp2(s_ref[i % SCORE_SLOTS, r:r + ROW_CHUNK, :] - m)
            lt = lt + p
            p_ref[i % 2, r:r + ROW_CHUNK, :] = p.astype(_bf16)
        l = jnp.sum(lt, axis=0, keepdims=True)
        o_t = _dot(vt_ref[:, 0:n], p_ref[i % 2, 0:n, :]) / l
        o_ref[0, i * MOBA_BLOCK:n, :] = jnp.concatenate(
            [o_t[:, h * MOBA_BLOCK:(h + 1) * MOBA_BLOCK].T for h in range(KV_GROUP)], axis=1)

    ahead = SCORE_SLOTS - 1
    for i in range(min(ahead, nb)):
        scores(i)

    def block(i):
        if i + ahead < nb:
            scores(i + ahead)
        finish(i)

    return block


def _slope_table(rows_per_head):
    slopes = 2.0 ** (-8.0 * jnp.arange(1, N_HEADS + 1, dtype=_f32) / N_HEADS)
    per_row = jnp.repeat(slopes.reshape(N_KV_HEADS, KV_GROUP), rows_per_head, axis=1)
    tab = jnp.zeros((N_KV_HEADS, SUBLANES, KV_GROUP * rows_per_head), _f32)
    return tab.at[:, 0, :].set(per_row)


def _sample_program(pt_ref, qbd_hi_ref, qbd_lo_ref, kn_ref, vn_ref, slope_row_ref, rtab_ref, ck_hbm, cv_hbm,
                    o_ref, pbuf, sem, s_ref, p_ref, ksum_ref, *, b, layer, nc, ds, nblk, n_seq):
    P = PAGES_PER_STEP
    njobs = 2 * nc
    QR = KV_GROUP * ds
    R = N_KV_HEADS * QR

    def chunk_copies(seq, job):
        src = ck_hbm if job < nc else cv_hbm
        slot = job % PAGE_SLOTS
        return [pltpu.make_async_copy(src.at[layer, pt_ref[seq, (job % nc) * P + pg]], pbuf.at[slot, pg], sem.at[slot])
                for pg in range(P)]

    def start(seq, job):
        for cp in chunk_copies(seq, job):
            cp.start()

    @pl.when(b == 0)
    def _():
        for job in range(PREFETCH_JOBS):
            start(b, job)

    def begin_job(job):
        for cp in chunk_copies(b, job):
            cp.wait()
        ahead = job + PREFETCH_JOBS
        if ahead < njobs:
            start(b, ahead)
        else:
            @pl.when(b + 1 < n_seq)
            def _():
                start(b + 1, ahead - njobs)
        return job % PAGE_SLOTS

    CW = P * PAGE_SIZE
    state = {}

    def to_rows(x_t):
        if x_t.shape[0] < LANES:
            x_t = jnp.concatenate([x_t, jnp.zeros((LANES - x_t.shape[0], LANES), _f32)], axis=0)
        return x_t.T[0:R]

    def key_job(c):
        slot = begin_job(c)
        rows, sums = [], [[] for _ in range(N_KV_HEADS)]
        for pg in range(P):
            heads = []
            for h in range(N_KV_HEADS):
                kp = pbuf[slot, pg, h]
                sums[h].append(jnp.sum(kp, axis=0, keepdims=True))
                heads.append(kp.astype(_bf16))
            rows.append(jnp.concatenate(heads, axis=1))
        s_ref[c * CW:(c + 1) * CW, :] = _dot(jnp.concatenate(rows, axis=0), qbd_hi_ref[0])
        ppb = MOBA_BLOCK // PAGE_SIZE
        for h in range(N_KV_HEADS):
            bs = [functools.reduce(lambda a, b: a + b, sums[h][j * ppb:(j + 1) * ppb]) for j in range(P // ppb)]
            ksum_ref[c * BLOCKS_PER_STEP:(c + 1) * BLOCKS_PER_STEP, h * HEAD_DIM:(h + 1) * HEAD_DIM] = (
                jnp.concatenate(bs, axis=0))

    def select():
        qbd_hi, qbd_lo = qbd_hi_ref[0], qbd_lo_ref[0]
        km_hi, km_lo = _split_bf16(ksum_ref[...] / MOBA_BLOCK)
        gate_t = _dot(km_hi, qbd_hi) + _dot(km_lo, qbd_hi) + _dot(km_hi, qbd_lo)
        jidx = lax.broadcasted_iota(jnp.int32, (nblk, LANES), 0)
        sel = _topk_select(gate_t, jidx < nblk, jidx, nblk, axis=0)
        slope = slope_row_ref[0:1, :]
        bias_t = jnp.where(sel, slope * (MOBA_BLOCK * (jidx - nblk)).astype(_f32), MASK_NEG)
        rtab = rtab_ref[...]

        mt = jnp.full((ROW_CHUNK, LANES), MASK_NEG, _f32)
        for j in range(nblk):
            blk = slice(j * MOBA_BLOCK, (j + 1) * MOBA_BLOCK)
            sc = s_ref[blk, :] + (rtab + bias_t[j:j + 1, :])
            s_ref[blk, :] = sc
            for r in range(0, MOBA_BLOCK, ROW_CHUNK):
                mt = jnp.maximum(mt, sc[r:r + ROW_CHUNK])
        xn = jnp.concatenate([kn_ref[0, h] for h in range(N_KV_HEADS)], axis=1)
        xn = jnp.concatenate([xn, jnp.zeros((2 * SUBLANES - ds, xn.shape[1]), _f32)], axis=0)
        so = _dot(xn.astype(_bf16), qbd_hi)[0:ds]
        ss = lax.broadcasted_iota(jnp.int32, (ds, LANES), 0)
        tt = lax.broadcasted_iota(jnp.int32, (ds, LANES), 1) % ds
        so = jnp.where(ss <= tt, so + slope * ss.astype(_f32), MASK_NEG)
        m = jnp.maximum(jnp.max(mt, axis=0, keepdims=True), jnp.max(so, axis=0, keepdims=True))
        po = jnp.exp2(so - m)
        l = jnp.sum(po, axis=0, keepdims=True)
        lt = jnp.zeros((PAGE_SIZE, LANES), _f32)
        for g in range(nblk * MOBA_BLOCK // PAGE_SIZE):
            p = jnp.exp2(s_ref[g * PAGE_SIZE:(g + 1) * PAGE_SIZE, :] - m)
            lt = lt + p
            col = (g * PAGE_SIZE) % CW
            p_ref[g * PAGE_SIZE // CW, :, col:col + PAGE_SIZE] = to_rows(p).astype(_bf16)
        l = l + jnp.sum(lt, axis=0, keepdims=True)
        state["l"] = to_rows(jnp.broadcast_to(l, (SUBLANES, LANES)))[:, 0:1]
        po_rows = to_rows(po)[:, 0:ds].astype(_bf16)
        state["acc"] = [_dot(po_rows[h * QR:(h + 1) * QR], vn_ref[0, h].astype(_bf16))
                        for h in range(N_KV_HEADS)]

    def value_job(c):
        slot = begin_job(nc + c)
        acc = state["acc"]
        p = p_ref[c]
        for h in range(N_KV_HEADS):
            vh = jnp.concatenate([pbuf[slot, pg, h].astype(_bf16) for pg in range(P)], axis=0)
            acc[h] = acc[h] + _dot(p[h * QR:(h + 1) * QR], vh)

    def write_out():
        for h in range(N_KV_HEADS):
            o = state["acc"][h] / state["l"][h * QR:(h + 1) * QR]
            for g in range(KV_GROUP):
                hh = KV_GROUP * h + g
                o_ref[0, :, hh * HEAD_DIM:(hh + 1) * HEAD_DIM] = o[g * ds:(g + 1) * ds]

    def job(j):
        if j < nc:
            key_job(j)
            if j == nc - 1:
                select()
        else:
            value_job(j - nc)
            if j == njobs - 1:
                write_out()

    return job


def _attn_kernel(pt_ref, qp_ref, kp_ref, vp_ref, pslope_ref,
                 qbd_hi_ref, qbd_lo_ref, kn_ref, vn_ref, slope_row_ref, rtab_ref, ck_hbm, cv_hbm,
                 op_ref, os_ref,
                 kaug_ref, vt_ref, ps_ref, pp_ref, pbuf, sem, ss_ref, sp_ref, ksum_ref,
                 *, nb, nbp, layer, nc, ds, nblk, n_seq):
    sample_job = _sample_program(pt_ref, qbd_hi_ref, qbd_lo_ref, kn_ref, vn_ref, slope_row_ref, rtab_ref,
                                 ck_hbm, cv_hbm, os_ref, pbuf, sem, ss_ref, sp_ref, ksum_ref,
                                 b=pl.program_id(0), layer=layer, nc=nc, ds=ds, nblk=nblk, n_seq=n_seq)
    prompt_block = _prompt_program(qp_ref, kp_ref, vp_ref, pslope_ref, op_ref, kaug_ref, vt_ref, ps_ref, pp_ref,
                                   nb=nb, nbp=nbp)
    jobs_per_block = 2 * nc // nb
    for i in range(nb):
        for j in range(i * jobs_per_block, (i + 1) * jobs_per_block):
            sample_job(j)
        prompt_block(i)


def _attention(qp, kp, vp, q, k_new, v_new, cache_k, cache_v, page_table, layer):
    B, S, _ = qp.shape
    assert S % MOBA_BLOCK == 0
    nb = S // MOBA_BLOCK
    nbp = -(-nb // SUBLANES) * SUBLANES
    assert N_PARTS * (nbp + SUBLANES) <= LANES and nb <= 2 * SUBLANES
    gw = KV_GROUP * HEAD_DIM
    PQR = KV_GROUP * MOBA_BLOCK

    DB, DS, _ = q.shape
    assert DB == B * N_KV_HEADS
    n_pages = page_table.shape[1]
    past = n_pages * PAGE_SIZE
    assert past % MOBA_BLOCK == 0 and DS <= MOBA_BLOCK and DS % SUBLANES == 0
    nblk = past // MOBA_BLOCK
    assert nblk >= MOBA_TOPK and nblk % SUBLANES == 0 and nblk + SUBLANES <= LANES
    P = PAGES_PER_STEP
    assert n_pages % P == 0 and P % (MOBA_BLOCK // PAGE_SIZE) == 0
    nc = n_pages // P
    QR = KV_GROUP * DS
    R = N_KV_HEADS * QR
    assert R <= LANES and QR % (2 * SUBLANES) == 0
    CW = P * PAGE_SIZE

    slopes = LOG2_E * 2.0 ** (-8.0 * jnp.arange(1, N_HEADS + 1, dtype=_f32) / N_HEADS)
    per_lane = jnp.zeros((LANES,), _f32).at[:R].set(jnp.repeat(slopes, DS))
    slope_row = jnp.zeros((SUBLANES, LANES), _f32).at[0].set(per_lane)
    rtab = jnp.arange(MOBA_BLOCK, dtype=_f32)[:, None] * per_lane[None, :]

    qh = (q * (ATTN_SCALE * LOG2_E)).reshape(DB, DS, N_KV_HEADS, KV_GROUP, HEAD_DIM).transpose(0, 2, 4, 3, 1)
    qh = qh.reshape(DB, N_KV_HEADS, HEAD_DIM, QR)
    qbd = jnp.einsum('bhdr,hk->bhdkr', qh, jnp.eye(N_KV_HEADS, dtype=_f32)).reshape(DB, KV_COLS, R)
    qbd = jnp.pad(qbd, ((0, 0), (0, 0), (0, LANES - R)))
    qbd_hi = qbd.astype(_bf16)
    qbd_lo = (qbd - qbd_hi.astype(_f32)).astype(_bf16)

    assert (2 * nc) % PAGE_SLOTS == 0 and PREFETCH_JOBS == PAGE_SLOTS - 1 and PREFETCH_JOBS <= 2 * nc

    assert (2 * nc) % nb == 0

    qo_spec = pl.BlockSpec((1, S, gw), lambda g, pt: (g // N_KV_HEADS, 0, g % N_KV_HEADS))
    kv_spec = pl.BlockSpec((1, 1, S, HEAD_DIM), lambda g, pt: (g // N_KV_HEADS, g % N_KV_HEADS, 0, 0))
    tok_spec = pl.BlockSpec((1, DS, Q_COLS), lambda g, pt: (g, 0, 0))
    new_spec = pl.BlockSpec((1, N_KV_HEADS, DS, HEAD_DIM), lambda g, pt: (g, 0, 0, 0))
    qbd_spec = pl.BlockSpec((1, KV_COLS, LANES), lambda g, pt: (g, 0, 0))
    hbm = pl.BlockSpec(memory_space=pl.ANY)
    in_specs = [qo_spec, kv_spec, kv_spec,
                pl.BlockSpec((1, SUBLANES, PQR), lambda g, pt: (g % N_KV_HEADS, 0, 0)),
                qbd_spec, qbd_spec, new_spec, new_spec,
                pl.BlockSpec((SUBLANES, LANES), lambda g, pt: (0, 0)),
                pl.BlockSpec((MOBA_BLOCK, LANES), lambda g, pt: (0, 0)),
                hbm, hbm]

    return pl.pallas_call(
        functools.partial(_attn_kernel, nb=nb, nbp=nbp, layer=layer, nc=nc, ds=DS, nblk=nblk, n_seq=DB),
        out_shape=[jax.ShapeDtypeStruct((B, S, Q_COLS), _f32), jax.ShapeDtypeStruct((DB, DS, Q_COLS), _f32)],
        grid_spec=pltpu.PrefetchScalarGridSpec(
            num_scalar_prefetch=1,
            grid=(DB,),
            in_specs=in_specs,
            out_specs=[qo_spec, tok_spec],
            scratch_shapes=[pltpu.VMEM((S, HEAD_DIM + LANES), _bf16),
                            pltpu.VMEM((HEAD_DIM, S), _bf16),
                            pltpu.VMEM((SCORE_SLOTS, S, PQR), _f32),
                            pltpu.VMEM((2, S, PQR), _bf16),
                            pltpu.VMEM((PAGE_SLOTS, P, N_KV_HEADS, PAGE_SIZE, HEAD_DIM), _f32),
                            pltpu.SemaphoreType.DMA((PAGE_SLOTS,)),
                            pltpu.VMEM((past, LANES), _f32),
                            pltpu.VMEM((nc, R, CW), _bf16),
                            pltpu.VMEM((nblk, KV_COLS), _f32)]),
        compiler_params=pltpu.CompilerParams(
            dimension_semantics=("arbitrary",), vmem_limit_bytes=VMEM_LIMIT_BYTES),
        name="attention",
    )(page_table, qp, kp, vp, _slope_table(MOBA_BLOCK), qbd_hi, qbd_lo, k_new, v_new, slope_row, rtab,
      cache_k, cache_v)


def _pool_windows_kernel(uext_ref, d_ref, *, ds, first_pos):
    cur = uext_ref[:, HIST_ROWS:HIST_ROWS + ds, :]
    pos = first_pos + lax.broadcasted_iota(jnp.int32, (1, ds, 1), 1)
    for g, w in enumerate(POOL_WINDOWS):
        sl = slice(g * POOL_GROUP_DIM, (g + 1) * POOL_GROUP_DIM)
        win = cur[:, :, sl]
        for dd in range(1, w):
            win = win + uext_ref[:, HIST_ROWS - dd:HIST_ROWS - dd + ds, sl]
        cnt = jnp.minimum(w, pos + 1).astype(_f32)
        d_ref[:, :, sl] = win / cnt - cur[:, :, sl]


def _pool_windows(u_ext_padded, ds, first_pos):
    DB, R, P = u_ext_padded.shape
    assert R == HIST_ROWS + ds
    bb = 16
    assert DB % bb == 0
    return pl.pallas_call(
        functools.partial(_pool_windows_kernel, ds=ds, first_pos=first_pos),
        out_shape=jax.ShapeDtypeStruct((DB, ds, P), _f32),
        grid=(DB // bb,),
        in_specs=[pl.BlockSpec((bb, R, P), lambda b: (b, 0, 0))],
        out_specs=pl.BlockSpec((bb, ds, P), lambda b: (b, 0, 0)),
        compiler_params=pltpu.CompilerParams(dimension_semantics=("arbitrary",)),
        name="pool_windows",
    )(u_ext_padded)


def _merge_ffn_kernel(x_ref, attn_ref, d_ref, sga_ref, sgb_ref, wpool_ref, pscale_ref, wout_ref, ln2_ref,
                      wgate_ref, wup_ref, wdown_ref, y_ref):
    d = d_ref[...]
    pool = jnp.concatenate(
        [_dot(d[:, g * POOL_GROUP_DIM:(g + 1) * POOL_GROUP_DIM].astype(_bf16), wpool_ref[g])
         for g in range(N_POOL_GROUPS)], axis=1) * pscale_ref[...]
    h = (sga_ref[...] * attn_ref[...] + sgb_ref[...] * pool).astype(_bf16)
    tm = h.shape[0]
    halves = [slice(0, tm // 2), slice(tm // 2, tm)]
    x1 = [x_ref[r, :] + _dot(h[r], wout_ref[...]) for r in halves]
    hn = [_rms(x1[a], ln2_ref[...]).astype(_bf16) for a in range(2)]
    gate = [None, None]
    up = [None, None]
    act = [None, None]
    gate[0] = _dot(hn[0], wgate_ref[...])
    up[0] = _dot(hn[0], wup_ref[...])
    gate[1] = _dot(hn[1], wgate_ref[...])
    act[0] = (gate[0] * _sigmoid(gate[0]) * up[0]).astype(_bf16)
    up[1] = _dot(hn[1], wup_ref[...])
    y_ref[halves[0], :] = x1[0] + _dot(act[0], wdown_ref[...])
    act[1] = (gate[1] * _sigmoid(gate[1]) * up[1]).astype(_bf16)
    y_ref[halves[1], :] = x1[1] + _dot(act[1], wdown_ref[...])


def _merge_ffn(x, attn, d, ga, gb, w_pool, pool_scale, w_out, ln2_g, w_gate, w_up, w_down):
    N, D = x.shape
    tm = min(TOKEN_TILE, N)
    assert N % tm == 0 and tm % SUBLANES == 0
    d_ff = w_gate.shape[1]
    tok = pl.BlockSpec((tm, D), lambda t: (t, 0))
    return pl.pallas_call(
        _merge_ffn_kernel,
        out_shape=jax.ShapeDtypeStruct((N, D), _f32),
        grid=(N // tm,),
        in_specs=[tok, tok, tok, tok, tok,
                  _resident((N_POOL_GROUPS, POOL_GROUP_DIM, POOL_GROUP_DIM)), _resident((1, POOL_WIDTH)),
                  _resident((D, D)), _resident((1, D)),
                  _resident((D, d_ff)), _resident((D, d_ff)), _resident((d_ff, D))],
        out_specs=tok,
        compiler_params=pltpu.CompilerParams(
            dimension_semantics=("arbitrary",), vmem_limit_bytes=VMEM_LIMIT_BYTES),
        name="merge_ffn",
    )(x, attn, d, ga, gb, w_pool, pool_scale.reshape(1, POOL_WIDTH), w_out, ln2_g.reshape(1, D),
      w_gate, w_up, w_down)


def kernel(x_prompt, x_sample, cache_k, cache_v, state_pool, page_table, ln1_g, w_in, q_norm_g, k_norm_g,
           w_pool, pool_scale, w_out, ln2_g, w_gate, w_up, w_down):
    depth = w_in.shape[0]
    B, S, D = x_prompt.shape
    DB, DS, _ = x_sample.shape
    past_len = page_table.shape[1] * PAGE_SIZE
    x_p, x_s = x_prompt, x_sample
    kp_l, vp_l, pp_l, ks_l, vs_l, ps_l = [], [], [], [], [], []
    for l in range(depth):
        w_in_b, w_pool_b, w_out_b = w_in[l].astype(_bf16), w_pool[l].astype(_bf16), w_out[l].astype(_bf16)
        w_gate_b, w_up_b, w_down_b = w_gate[l].astype(_bf16), w_up[l].astype(_bf16), w_down[l].astype(_bf16)
        ffn = functools.partial(_merge_ffn, w_pool=w_pool_b, pool_scale=pool_scale[l], w_out=w_out_b,
                                ln2_g=ln2_g[l], w_gate=w_gate_b, w_up=w_up_b, w_down=w_down_b)

        qp, kp, vp, dp, gap, gbp, hist = _proj(x_p, ln1_g[l], w_in_b, q_norm_g[l], k_norm_g[l], windows=True)
        q, k, v, u, ga, gb = _proj(x_s.reshape(1, DB * DS, D), ln1_g[l], w_in_b, q_norm_g[l], k_norm_g[l],
                                   windows=False)
        per_seq = lambda a: a.reshape(N_KV_HEADS, DB, DS, HEAD_DIM).transpose(1, 0, 2, 3)
        k, v = per_seq(k), per_seq(v)

        attn_p, attn = _attention(qp, kp, vp, q.reshape(DB, DS, Q_COLS), k, v, cache_k, cache_v, page_table, l)

        flat = lambda a: a.reshape(B * S, a.shape[-1])
        x_p = ffn(flat(x_p), flat(attn_p), flat(dp), flat(gap), flat(gbp)).reshape(B, S, D)
        kp_l.append(kp)
        vp_l.append(vp)
        pp_l.append(hist[:, HIST_ROWS - POOL_HIST:])

        u_ext = jnp.concatenate(
            [jnp.zeros((DB, HIST_ROWS - POOL_HIST, POOL_WIDTH), _f32), state_pool[l], u.reshape(DB, DS, POOL_WIDTH)],
            axis=1)
        d = _pool_windows(u_ext, DS, past_len)
        flat = lambda a: a.reshape(DB * DS, a.shape[-1])
        x_s = ffn(flat(x_s), flat(attn), flat(d), flat(ga), flat(gb)).reshape(DB, DS, D)
        ks_l.append(k)
        vs_l.append(v)
        ps_l.append(u_ext[:, HIST_ROWS + DS - POOL_HIST:])
    return (x_p, x_s, jnp.stack(kp_l), jnp.stack(vp_l), jnp.stack(pp_l),
            jnp.stack(ks_l), jnp.stack(vs_l), jnp.stack(ps_l))
```

```python
import functools

import jax
import jax.numpy as jnp
from jax import lax
from jax.experimental import pallas as pl
from jax.experimental.pallas import tpu as pltpu

D_MODEL = 1024
N_HEADS = 8
HEAD_DIM = D_MODEL // N_HEADS
N_KV_HEADS = 4
KV_GROUP = N_HEADS // N_KV_HEADS
MOBA_BLOCK = 256
MOBA_TOPK = 3
PAGE_SIZE = 128
ATTN_SCALE = HEAD_DIM ** -0.5
POOL_WINDOWS = (2, 4, 8, 16)
N_POOL_GROUPS = len(POOL_WINDOWS)
POOL_WIDTH = D_MODEL
POOL_GROUP_DIM = POOL_WIDTH // N_POOL_GROUPS
POOL_HIST = max(POOL_WINDOWS) - 1
RMS_EPS = 1e-6
Q_COLS = N_HEADS * HEAD_DIM
KV_COLS = N_KV_HEADS * HEAD_DIM
COL_Q = 0
COL_K = Q_COLS
COL_V = COL_K + KV_COLS
COL_U = COL_V + KV_COLS
COL_GA = COL_U + POOL_WIDTH
COL_GB = COL_GA + D_MODEL
IN_COLS = COL_GB + D_MODEL

SUBLANES = 8
LANES = 128
MXU_DIM = 256
VMEM_LIMIT_BYTES = 56 * 1024 * 1024

HIST_ROWS = 16
assert HIST_ROWS >= POOL_HIST and HIST_ROWS % SUBLANES == 0
MASK_NEG = -(2.0 ** 30)
LOG2_E = 1.4426950408889634

TOKEN_TILE = 256
ROW_CHUNK = 32
SCORE_SLOTS = 3
PAGES_PER_STEP = 16
BLOCKS_PER_STEP = PAGES_PER_STEP * PAGE_SIZE // MOBA_BLOCK
PAGE_SLOTS = 4
PREFETCH_JOBS = PAGE_SLOTS - 1

_f32 = jnp.float32
_bf16 = jnp.bfloat16


def _nt_dot(a, b, precision=None):
    return lax.dot_general(a, b, (((1,), (1,)), ((), ())), precision=precision,
                           preferred_element_type=_f32)


def _dot(a, b):
    return jnp.dot(a, b, preferred_element_type=_f32)


def _rms(x, g):
    return x * lax.rsqrt(jnp.mean(x * x, axis=-1, keepdims=True) + RMS_EPS) * g


def _sigmoid(x):
    return 1.0 / (1.0 + jnp.exp(-x))


def _resident(shape):
    nd = len(shape)
    return pl.BlockSpec(shape, lambda *_: (0,) * nd, pipeline_mode=pl.Buffered(1))


def _proj_kernel(x_ref, ln1_ref, w_ref, qg_ref, kg_ref,
                 q_ref, k_ref, v_ref, u_ref, ga_ref, gb_ref, *rest, tm, windows):
    if windows:
        hist_ref, carry_ref = rest
        si = pl.program_id(1)

        @pl.when(si == 0)
        def _():
            carry_ref[...] = jnp.zeros((HIST_ROWS, POOL_WIDTH), _f32)

    xn = _rms(x_ref[0], ln1_ref[...]).astype(_bf16)

    zu = _dot(xn, w_ref[:, COL_U:COL_GA])
    ga_ref[0] = _sigmoid(_dot(xn, w_ref[:, COL_GA:COL_GB]))
    gb_ref[0] = _sigmoid(_dot(xn, w_ref[:, COL_GB:IN_COLS]))
    if windows:
        pos = si * tm + lax.broadcasted_iota(jnp.int32, (tm, 1), 0)
        for g, w in enumerate(POOL_WINDOWS):
            sl = slice(g * POOL_GROUP_DIM, (g + 1) * POOL_GROUP_DIM)
            run = jnp.concatenate([carry_ref[:, sl], zu[:, sl]], axis=0)
            half = 1
            while half < w:
                run = run + pltpu.roll(run, half, axis=0)
                half *= 2
            cnt = jnp.minimum(w, pos + 1).astype(_f32)
            u_ref[0, :, sl] = run[HIST_ROWS:] / cnt - zu[:, sl]
        hist_ref[0] = zu[tm - HIST_ROWS:tm, :]
        carry_ref[...] = zu[tm - HIST_ROWS:tm, :]
    else:
        u_ref[0] = zu

    zq = _dot(xn, w_ref[:, COL_Q:COL_K])
    qg = qg_ref[...]
    for h in range(N_HEADS):
        sl = slice(h * HEAD_DIM, (h + 1) * HEAD_DIM)
        q_ref[0, :, sl] = _rms(zq[:, sl], qg)

    zk = _dot(xn, w_ref[:, COL_K:COL_V])
    zv = _dot(xn, w_ref[:, COL_V:COL_U])
    kg = kg_ref[...]
    for h in range(N_KV_HEADS):
        sl = slice(h * HEAD_DIM, (h + 1) * HEAD_DIM)
        k_ref[0, h] = _rms(zk[:, sl], kg)
        v_ref[0, h] = zv[:, sl]


def _proj(x, ln1_g, w_in_bf16, q_norm_g, k_norm_g, *, windows):
    B, S, D = x.shape
    tm = min(TOKEN_TILE, S)
    assert S % tm == 0 and tm % HIST_ROWS == 0
    tok = lambda n: pl.BlockSpec((1, tm, n), lambda b, s: (b, s, 0))
    kv = pl.BlockSpec((1, N_KV_HEADS, tm, HEAD_DIM), lambda b, s: (b, 0, s, 0))
    out_shape = [
        jax.ShapeDtypeStruct((B, S, Q_COLS), _f32),
        jax.ShapeDtypeStruct((B, N_KV_HEADS, S, HEAD_DIM), _f32),
        jax.ShapeDtypeStruct((B, N_KV_HEADS, S, HEAD_DIM), _f32),
        jax.ShapeDtypeStruct((B, S, POOL_WIDTH), _f32),
        jax.ShapeDtypeStruct((B, S, D_MODEL), _f32),
        jax.ShapeDtypeStruct((B, S, D_MODEL), _f32),
    ]
    out_specs = [tok(Q_COLS), kv, kv, tok(POOL_WIDTH), tok(D_MODEL), tok(D_MODEL)]
    scratch = []
    if windows:
        out_shape.append(jax.ShapeDtypeStruct((B, HIST_ROWS, POOL_WIDTH), _f32))
        out_specs.append(pl.BlockSpec((1, HIST_ROWS, POOL_WIDTH), lambda b, s: (b, 0, 0)))
        scratch.append(pltpu.VMEM((HIST_ROWS, POOL_WIDTH), _f32))
    return pl.pallas_call(
        functools.partial(_proj_kernel, tm=tm, windows=windows),
        out_shape=out_shape,
        grid=(B, S // tm),
        in_specs=[tok(D), _resident((1, D)), _resident((D, IN_COLS)),
                  _resident((1, HEAD_DIM)), _resident((1, HEAD_DIM))],
        out_specs=out_specs,
        scratch_shapes=scratch,
        compiler_params=pltpu.CompilerParams(
            dimension_semantics=("arbitrary", "arbitrary"), vmem_limit_bytes=VMEM_LIMIT_BYTES),
        name="proj_windows" if windows else "proj",
    )(x, ln1_g.reshape(1, D), w_in_bf16, q_norm_g.reshape(1, HEAD_DIM), k_norm_g.reshape(1, HEAD_DIM))


def _topk_select(g, valid, row_idx, n_rows, axis):
    gm = jnp.where(valid, g, -jnp.inf)
    cnt = jnp.zeros(g.shape, jnp.int32)
    for jp in range(n_rows):
        gj = lax.slice_in_dim(gm, jp, jp + 1, axis=axis)
        beats = (gj > gm) | ((gj == gm) & (jp < row_idx))
        cnt = cnt + jnp.where(beats, 1, 0)
    return valid & (cnt < MOBA_TOPK)


def _split_bf16(x):
    hi = x.astype(_bf16)
    return hi, (x - hi.astype(_f32)).astype(_bf16)


N_PARTS = 3


def _split3(x):
    parts = []
    for _ in range(N_PARTS):
        part = x.astype(_bf16).astype(_f32)
        parts.append(part)
        x = x - part
    return parts


def _prompt_program(q_ref, k_ref, v_ref, slope_ref, o_ref, kaug_ref, vt_ref, s_ref, p_ref, *,
                    first_step, nb, nbp):
    S = nb * MOBA_BLOCK
    QR = KV_GROUP * MOBA_BLOCK
    KM_ROWS = 2 * SUBLANES

    k = k_ref[0, 0]
    kaug_ref[:, 0:HEAD_DIM] = k.astype(_bf16)
    @pl.when(first_step)
    def _():
        row = lax.broadcasted_iota(jnp.int32, (S, LANES), 0)
        lane = lax.broadcasted_iota(jnp.int32, (S, LANES), 1)
        onehot = jnp.where((lane % nbp == row // MOBA_BLOCK) & (lane < N_PARTS * nbp), 1.0, 0.0)
        is_rpos = (lane >= N_PARTS * nbp) & (lane < N_PARTS * (nbp + SUBLANES)) & (lane % SUBLANES == 0)
        rpos = jnp.where(is_rpos, (row % MOBA_BLOCK).astype(_f32), 0.0)
        kaug_ref[:, HEAD_DIM:HEAD_DIM + LANES] = (onehot + rpos).astype(_bf16)

    vt_ref[...] = v_ref[0, 0].T.astype(_bf16)

    kmean = jnp.mean(k.reshape(nb, MOBA_BLOCK, HEAD_DIM), axis=1)
    km = jnp.concatenate([kmean, jnp.zeros((KM_ROWS - nb, HEAD_DIM), _f32)], axis=0)
    km_hi, km_lo = _split_bf16(km)
    km_cat = jnp.concatenate([km_hi, km_hi, km_lo], axis=1)

    slope_rows = slope_ref[0]
    slope = slope_rows[0:1, :]
    jidx = lax.broadcasted_iota(jnp.int32, (nbp, QR), 0)
    key_r = lax.broadcasted_iota(jnp.int32, (MOBA_BLOCK, QR), 0)
    t_rel = lax.broadcasted_iota(jnp.int32, (MOBA_BLOCK, QR), 1) % MOBA_BLOCK
    causal = key_r <= t_rel

    def augmented_queries(i):
        rows = slice(i * MOBA_BLOCK, (i + 1) * MOBA_BLOCK)
        q2 = q_ref[0, rows, :]
        q_t = jnp.concatenate([q2[:, h * HEAD_DIM:(h + 1) * HEAD_DIM].T for h in range(KV_GROUP)], axis=1)

        if i > MOBA_TOPK:
            q_hi, q_lo = _split_bf16(q_t)
            gate_t = _dot(km_cat, jnp.concatenate([q_hi, q_lo, q_hi], axis=0))[0:nbp]
            sel = _topk_select(gate_t, jidx < i, jidx, i, axis=0)
        else:
            sel = jidx < i
        bias_t = jnp.where(sel, slope * (MOBA_BLOCK * (jidx - i)).astype(_f32), MASK_NEG)
        bias_t = jnp.where(jidx == i, 0.0, bias_t)
        aux_t = jnp.concatenate(
            _split3(bias_t * LOG2_E) + _split3(slope_rows * LOG2_E)
            + [jnp.zeros((LANES - N_PARTS * (nbp + SUBLANES), QR), _f32)], axis=0)
        return jnp.concatenate([(q_t * (ATTN_SCALE * LOG2_E)).astype(_bf16), aux_t.astype(_bf16)], axis=0)

    row_max = {}

    def scores(i):
        qa_t = augmented_queries(i)
        mt = None
        for j in range(i + 1):
            blk = slice(j * MOBA_BLOCK, (j + 1) * MOBA_BLOCK)
            sj = _dot(kaug_ref[blk, :], qa_t)
            if j == i:
                sj = jnp.where(causal, sj, MASK_NEG)
            s_ref[i % SCORE_SLOTS, blk, :] = sj
            for r in range(0, MOBA_BLOCK, ROW_CHUNK):
                mt = sj[r:r + ROW_CHUNK] if mt is None else jnp.maximum(mt, sj[r:r + ROW_CHUNK])
        row_max[i] = jnp.max(mt, axis=0, keepdims=True)

    def finish(i):
        m = row_max.pop(i)
        n = (i + 1) * MOBA_BLOCK
        lt = jnp.zeros((ROW_CHUNK, QR), _f32)
        for r in range(0, n, ROW_CHUNK):
            p = jnp.exp2(s_ref[i % SCORE_SLOTS, r:r + ROW_CHUNK, :] - m)
            lt = lt + p
            p_ref[i % 2, r:r + ROW_CHUNK, :] = p.astype(_bf16)
        l = jnp.sum(lt, axis=0, keepdims=True)
        o_t = _dot(vt_ref[:, 0:n], p_ref[i % 2, 0:n, :]) / l
        o_ref[0, i * MOBA_BLOCK:n, :] = jnp.concatenate(
            [o_t[:, h * MOBA_BLOCK:(h + 1) * MOBA_BLOCK].T for h in range(KV_GROUP)], axis=1)

    ahead = SCORE_SLOTS - 1
    for i in range(min(ahead, nb)):
        scores(i)

    def block(i):
        if i + ahead < nb:
            scores(i + ahead)
        finish(i)

    return block


def _slope_table(rows_per_head):
    slopes = 2.0 ** (-8.0 * jnp.arange(1, N_HEADS + 1, dtype=_f32) / N_HEADS)
    per_row = jnp.repeat(slopes.reshape(N_KV_HEADS, KV_GROUP), rows_per_head, axis=1)
    tab = jnp.zeros((N_KV_HEADS, SUBLANES, KV_GROUP * rows_per_head), _f32)
    return tab.at[:, 0, :].set(per_row)


def _sample_program(pt_ref, qbd_hi_ref, qbd_lo_ref, kn_ref, vn_ref, slope_row_ref, rtab_ref, ck_hbm, cv_hbm,
                    o_ref, pbuf, sem, s_ref, p_ref, ksum_ref, bmax_ref, *, b, layer, nc, ds, nblk, n_seq):
    P = PAGES_PER_STEP
    njobs = 2 * nc
    QR = KV_GROUP * ds
    R = N_KV_HEADS * QR

    def chunk_copies(seq, job):
        src = ck_hbm if job < nc else cv_hbm
        slot = job % PAGE_SLOTS
        return [pltpu.make_async_copy(src.at[layer, pt_ref[seq, (job % nc) * P + pg]], pbuf.at[slot, pg], sem.at[slot])
                for pg in range(P)]

    def start(seq, job):
        for cp in chunk_copies(seq, job):
            cp.start()

    @pl.when(b == 0)
    def _():
        for job in range(PREFETCH_JOBS):
            start(b, job)

    def begin_job(job):
        for cp in chunk_copies(b, job):
            cp.wait()
        ahead = job + PREFETCH_JOBS
        if ahead < njobs:
            start(b, ahead)
        else:
            @pl.when(b + 1 < n_seq)
            def _():
                start(b + 1, ahead - njobs)
        return job % PAGE_SLOTS

    CW = P * PAGE_SIZE
    state = {}

    def to_rows(x_t):
        if x_t.shape[0] < LANES:
            x_t = jnp.concatenate([x_t, jnp.zeros((LANES - x_t.shape[0], LANES), _f32)], axis=0)
        return x_t.T[0:R]

    def key_job(c):
        slot = begin_job(c)
        rows, sums = [], [[] for _ in range(N_KV_HEADS)]
        for pg in range(P):
            heads = []
            for h in range(N_KV_HEADS):
                kp = pbuf[slot, pg, h]
                sums[h].append(jnp.sum(kp, axis=0, keepdims=True))
                heads.append(kp.astype(_bf16))
            rows.append(jnp.concatenate(heads, axis=1))
        s = _dot(jnp.concatenate(rows, axis=0), qbd_hi_ref[0])
        rtab = rtab_ref[...]
        for jb in range(BLOCKS_PER_STEP):
            j = c * BLOCKS_PER_STEP + jb
            sb = s[jb * MOBA_BLOCK:(jb + 1) * MOBA_BLOCK] + rtab
            s_ref[j * MOBA_BLOCK:(j + 1) * MOBA_BLOCK, :] = sb
            bmax_ref[j:j + 1, :] = jnp.max(sb, axis=0, keepdims=True)
        ppb = MOBA_BLOCK // PAGE_SIZE
        for h in range(N_KV_HEADS):
            bs = [functools.reduce(lambda a, b: a + b, sums[h][j * ppb:(j + 1) * ppb]) for j in range(P // ppb)]
            ksum_ref[c * BLOCKS_PER_STEP:(c + 1) * BLOCKS_PER_STEP, h * HEAD_DIM:(h + 1) * HEAD_DIM] = (
                jnp.concatenate(bs, axis=0))

    def select():
        qbd_hi, qbd_lo = qbd_hi_ref[0], qbd_lo_ref[0]
        km_hi, km_lo = _split_bf16(ksum_ref[...] / MOBA_BLOCK)
        gate_t = _dot(km_hi, qbd_hi) + _dot(km_lo, qbd_hi) + _dot(km_hi, qbd_lo)
        jidx = lax.broadcasted_iota(jnp.int32, (nblk, LANES), 0)
        sel = _topk_select(gate_t, jidx < nblk, jidx, nblk, axis=0)
        slope = slope_row_ref[0:1, :]
        bias_t = jnp.where(sel, slope * (MOBA_BLOCK * (jidx - nblk)).astype(_f32), MASK_NEG)
        mt = bmax_ref[...] + bias_t
        xn = jnp.concatenate([kn_ref[0, h] for h in range(N_KV_HEADS)], axis=1)
        xn = jnp.concatenate([xn, jnp.zeros((2 * SUBLANES - ds, xn.shape[1]), _f32)], axis=0)
        so = _dot(xn.astype(_bf16), qbd_hi)[0:ds]
        ss = lax.broadcasted_iota(jnp.int32, (ds, LANES), 0)
        tt = lax.broadcasted_iota(jnp.int32, (ds, LANES), 1) % ds
        so = jnp.where(ss <= tt, so + slope * ss.astype(_f32), MASK_NEG)
        m = jnp.maximum(jnp.max(mt, axis=0, keepdims=True), jnp.max(so, axis=0, keepdims=True))
        po = jnp.exp2(so - m)
        l = jnp.sum(po, axis=0, keepdims=True)
        shift = bias_t - m
        lt = jnp.zeros((PAGE_SIZE, LANES), _f32)
        for g in range(nblk * MOBA_BLOCK // PAGE_SIZE):
            j = g * PAGE_SIZE // MOBA_BLOCK
            p = jnp.exp2(s_ref[g * PAGE_SIZE:(g + 1) * PAGE_SIZE, :] + shift[j:j + 1, :])
            lt = lt + p
            col = (g * PAGE_SIZE) % CW
            p_ref[g * PAGE_SIZE // CW, :, col:col + PAGE_SIZE] = to_rows(p).astype(_bf16)
        l = l + jnp.sum(lt, axis=0, keepdims=True)
        state["l"] = to_rows(jnp.broadcast_to(l, (SUBLANES, LANES)))[:, 0:1]
        po_rows = to_rows(po)[:, 0:ds].astype(_bf16)
        state["acc"] = [_dot(po_rows[h * QR:(h + 1) * QR], vn_ref[0, h].astype(_bf16))
                        for h in range(N_KV_HEADS)]

    def value_job(c):
        slot = begin_job(nc + c)
        acc = state["acc"]
        p = p_ref[c]
        for h in range(N_KV_HEADS):
            vh = jnp.concatenate([pbuf[slot, pg, h].astype(_bf16) for pg in range(P)], axis=0)
            acc[h] = acc[h] + _dot(p[h * QR:(h + 1) * QR], vh)

    def write_out():
        for h in range(N_KV_HEADS):
            o = state["acc"][h] / state["l"][h * QR:(h + 1) * QR]
            for g in range(KV_GROUP):
                hh = KV_GROUP * h + g
                o_ref[0, :, hh * HEAD_DIM:(hh + 1) * HEAD_DIM] = o[g * ds:(g + 1) * ds]

    def job(j):
        if j < nc:
            key_job(j)
            if j == nc - 1:
                select()
        else:
            value_job(j - nc)
            if j == njobs - 1:
                write_out()

    return job


def _attn_kernel(pt_ref, qp_ref, kp_ref, vp_ref, pslope_ref,
                 qbd_hi_ref, qbd_lo_ref, kn_ref, vn_ref, slope_row_ref, rtab_ref, ck_hbm, cv_hbm,
                 op_ref, os_ref,
                 kaug_ref, vt_ref, ps_ref, pp_ref, pbuf, sem, ss_ref, sp_ref, ksum_ref, bmax_ref,
                 *, nb, nbp, layer, nc, ds, nblk, n_seq):
    sample_job = _sample_program(pt_ref, qbd_hi_ref, qbd_lo_ref, kn_ref, vn_ref, slope_row_ref, rtab_ref,
                                 ck_hbm, cv_hbm, os_ref, pbuf, sem, ss_ref, sp_ref, ksum_ref, bmax_ref,
                                 b=pl.program_id(0), layer=layer, nc=nc, ds=ds, nblk=nblk, n_seq=n_seq)
    prompt_block = _prompt_program(qp_ref, kp_ref, vp_ref, pslope_ref, op_ref, kaug_ref, vt_ref, ps_ref, pp_ref,
                                   first_step=pl.program_id(0) == 0, nb=nb, nbp=nbp)
    jobs_per_block = 2 * nc // nb
    for i in range(nb):
        for j in range(i * jobs_per_block, (i + 1) * jobs_per_block):
            sample_job(j)
        prompt_block(i)


def _attention(qp, kp, vp, q, k_new, v_new, cache_k, cache_v, page_table, layer):
    B, S, _ = qp.shape
    assert S % MOBA_BLOCK == 0
    nb = S // MOBA_BLOCK
    nbp = -(-nb // SUBLANES) * SUBLANES
    assert N_PARTS * (nbp + SUBLANES) <= LANES and nb <= 2 * SUBLANES
    gw = KV_GROUP * HEAD_DIM
    PQR = KV_GROUP * MOBA_BLOCK

    DB, DS, _ = q.shape
    assert DB == B * N_KV_HEADS
    n_pages = page_table.shape[1]
    past = n_pages * PAGE_SIZE
    assert past % MOBA_BLOCK == 0 and DS <= MOBA_BLOCK and DS % SUBLANES == 0
    nblk = past // MOBA_BLOCK
    assert nblk >= MOBA_TOPK and nblk % SUBLANES == 0 and nblk + SUBLANES <= LANES
    P = PAGES_PER_STEP
    assert n_pages % P == 0 and P % (MOBA_BLOCK // PAGE_SIZE) == 0
    nc = n_pages // P
    QR = KV_GROUP * DS
    R = N_KV_HEADS * QR
    assert R <= LANES and QR % (2 * SUBLANES) == 0
    CW = P * PAGE_SIZE

    slopes = LOG2_E * 2.0 ** (-8.0 * jnp.arange(1, N_HEADS + 1, dtype=_f32) / N_HEADS)
    per_lane = jnp.zeros((LANES,), _f32).at[:R].set(jnp.repeat(slopes, DS))
    slope_row = jnp.zeros((SUBLANES, LANES), _f32).at[0].set(per_lane)
    rtab = jnp.arange(MOBA_BLOCK, dtype=_f32)[:, None] * per_lane[None, :]

    def embed(part):
        qh = part.reshape(DB, DS, N_KV_HEADS, KV_GROUP, HEAD_DIM).transpose(0, 2, 4, 3, 1)
        qh = qh.reshape(DB, N_KV_HEADS, HEAD_DIM, 1, QR)
        own_head = jnp.eye(N_KV_HEADS, dtype=jnp.bool_)[None, :, None, :, None]
        qbd = jnp.where(own_head, qh, jnp.zeros((), _bf16)).reshape(DB, KV_COLS, R)
        return jnp.pad(qbd, ((0, 0), (0, 0), (0, LANES - R)))

    q_scaled = q * (ATTN_SCALE * LOG2_E)
    q_hi = q_scaled.astype(_bf16)
    qbd_hi = embed(q_hi)
    qbd_lo = embed((q_scaled - q_hi.astype(_f32)).astype(_bf16))

    assert (2 * nc) % PAGE_SLOTS == 0 and PREFETCH_JOBS == PAGE_SLOTS - 1 and PREFETCH_JOBS <= 2 * nc

    assert (2 * nc) % nb == 0

    qo_spec = pl.BlockSpec((1, S, gw), lambda g, pt: (g // N_KV_HEADS, 0, g % N_KV_HEADS))
    kv_spec = pl.BlockSpec((1, 1, S, HEAD_DIM), lambda g, pt: (g // N_KV_HEADS, g % N_KV_HEADS, 0, 0))
    tok_spec = pl.BlockSpec((1, DS, Q_COLS), lambda g, pt: (g, 0, 0))
    new_spec = pl.BlockSpec((1, N_KV_HEADS, DS, HEAD_DIM), lambda g, pt: (g, 0, 0, 0))
    qbd_spec = pl.BlockSpec((1, KV_COLS, LANES), lambda g, pt: (g, 0, 0))
    hbm = pl.BlockSpec(memory_space=pl.ANY)
    in_specs = [qo_spec, kv_spec, kv_spec,
                pl.BlockSpec((1, SUBLANES, PQR), lambda g, pt: (g % N_KV_HEADS, 0, 0)),
                qbd_spec, qbd_spec, new_spec, new_spec,
                pl.BlockSpec((SUBLANES, LANES), lambda g, pt: (0, 0)),
                pl.BlockSpec((MOBA_BLOCK, LANES), lambda g, pt: (0, 0)),
                hbm, hbm]

    return pl.pallas_call(
        functools.partial(_attn_kernel, nb=nb, nbp=nbp, layer=layer, nc=nc, ds=DS, nblk=nblk, n_seq=DB),
        out_shape=[jax.ShapeDtypeStruct((B, S, Q_COLS), _f32), jax.ShapeDtypeStruct((DB, DS, Q_COLS), _f32)],
        grid_spec=pltpu.PrefetchScalarGridSpec(
            num_scalar_prefetch=1,
            grid=(DB,),
            in_specs=in_specs,
            out_specs=[qo_spec, tok_spec],
            scratch_shapes=[pltpu.VMEM((S, HEAD_DIM + LANES), _bf16),
                            pltpu.VMEM((HEAD_DIM, S), _bf16),
                            pltpu.VMEM((SCORE_SLOTS, S, PQR), _f32),
                            pltpu.VMEM((2, S, PQR), _bf16),
                            pltpu.VMEM((PAGE_SLOTS, P, N_KV_HEADS, PAGE_SIZE, HEAD_DIM), _f32),
                            pltpu.SemaphoreType.DMA((PAGE_SLOTS,)),
                            pltpu.VMEM((past, LANES), _f32),
                            pltpu.VMEM((nc, R, CW), _bf16),
                            pltpu.VMEM((nblk, KV_COLS), _f32),
                            pltpu.VMEM((nblk, LANES), _f32)]),
        compiler_params=pltpu.CompilerParams(
            dimension_semantics=("arbitrary",), vmem_limit_bytes=VMEM_LIMIT_BYTES),
        name="attention",
    )(page_table, qp, kp, vp, _slope_table(MOBA_BLOCK), qbd_hi, qbd_lo, k_new, v_new, slope_row, rtab,
      cache_k, cache_v)


def _pool_windows_kernel(uext_ref, d_ref, *, ds, first_pos):
    cur = uext_ref[:, HIST_ROWS:HIST_ROWS + ds, :]
    pos = first_pos + lax.broadcasted_iota(jnp.int32, (1, ds, 1), 1)
    for g, w in enumerate(POOL_WINDOWS):
        sl = slice(g * POOL_GROUP_DIM, (g + 1) * POOL_GROUP_DIM)
        win = cur[:, :, sl]
        for dd in range(1, w):
            win = win + uext_ref[:, HIST_ROWS - dd:HIST_ROWS - dd + ds, sl]
        cnt = jnp.minimum(w, pos + 1).astype(_f32)
        d_ref[:, :, sl] = win / cnt - cur[:, :, sl]


def _pool_windows(u_ext_padded, ds, first_pos):
    DB, R, P = u_ext_padded.shape
    assert R == HIST_ROWS + ds
    bb = 16
    assert DB % bb == 0
    return pl.pallas_call(
        functools.partial(_pool_windows_kernel, ds=ds, first_pos=first_pos),
        out_shape=jax.ShapeDtypeStruct((DB, ds, P), _f32),
        grid=(DB // bb,),
        in_specs=[pl.BlockSpec((bb, R, P), lambda b: (b, 0, 0))],
        out_specs=pl.BlockSpec((bb, ds, P), lambda b: (b, 0, 0)),
        compiler_params=pltpu.CompilerParams(dimension_semantics=("arbitrary",)),
        name="pool_windows",
    )(u_ext_padded)


def _merge_ffn_kernel(x_ref, attn_ref, d_ref, sga_ref, sgb_ref, wpool_ref, pscale_ref, wout_ref, ln2_ref,
                      wgate_ref, wup_ref, wdown_ref, y_ref):
    d = d_ref[...]
    pool = jnp.concatenate(
        [_dot(d[:, g * POOL_GROUP_DIM:(g + 1) * POOL_GROUP_DIM].astype(_bf16), wpool_ref[g])
         for g in range(N_POOL_GROUPS)], axis=1) * pscale_ref[...]
    h = (sga_ref[...] * attn_ref[...] + sgb_ref[...] * pool).astype(_bf16)
    tm = h.shape[0]
    halves = [slice(0, tm // 2), slice(tm // 2, tm)]
    x1 = [x_ref[r, :] + _dot(h[r], wout_ref[...]) for r in halves]
    hn = [_rms(x1[a], ln2_ref[...]).astype(_bf16) for a in range(2)]
    gate = [None, None]
    up = [None, None]
    act = [None, None]
    gate[0] = _dot(hn[0], wgate_ref[...])
    up[0] = _dot(hn[0], wup_ref[...])
    gate[1] = _dot(hn[1], wgate_ref[...])
    act[0] = (gate[0] * _sigmoid(gate[0]) * up[0]).astype(_bf16)
    up[1] = _dot(hn[1], wup_ref[...])
    y_ref[halves[0], :] = x1[0] + _dot(act[0], wdown_ref[...])
    act[1] = (gate[1] * _sigmoid(gate[1]) * up[1]).astype(_bf16)
    y_ref[halves[1], :] = x1[1] + _dot(act[1], wdown_ref[...])


def _merge_ffn(x, attn, d, ga, gb, w_pool, pool_scale, w_out, ln2_g, w_gate, w_up, w_down):
    N, D = x.shape
    tm = min(TOKEN_TILE, N)
    assert N % tm == 0 and tm % SUBLANES == 0
    d_ff = w_gate.shape[1]
    tok = pl.BlockSpec((tm, D), lambda t: (t, 0))
    return pl.pallas_call(
        _merge_ffn_kernel,
        out_shape=jax.ShapeDtypeStruct((N, D), _f32),
        grid=(N // tm,),
        in_specs=[tok, tok, tok, tok, tok,
                  _resident((N_POOL_GROUPS, POOL_GROUP_DIM, POOL_GROUP_DIM)), _resident((1, POOL_WIDTH)),
                  _resident((D, D)), _resident((1, D)),
                  _resident((D, d_ff)), _resident((D, d_ff)), _resident((d_ff, D))],
        out_specs=tok,
        compiler_params=pltpu.CompilerParams(
            dimension_semantics=("arbitrary",), vmem_limit_bytes=VMEM_LIMIT_BYTES),
        name="merge_ffn",
    )(x, attn, d, ga, gb, w_pool, pool_scale.reshape(1, POOL_WIDTH), w_out, ln2_g.reshape(1, D),
      w_gate, w_up, w_down)


def kernel(x_prompt, x_sample, cache_k, cache_v, state_pool, page_table, ln1_g, w_in, q_norm_g, k_norm_g,
           w_pool, pool_scale, w_out, ln2_g, w_gate, w_up, w_down):
    depth = w_in.shape[0]
    B, S, D = x_prompt.shape
    DB, DS, _ = x_sample.shape
    past_len = page_table.shape[1] * PAGE_SIZE
    x_p, x_s = x_prompt, x_sample
    kp_l, vp_l, pp_l, ks_l, vs_l, ps_l = [], [], [], [], [], []
    for l in range(depth):
        w_in_b, w_pool_b, w_out_b = w_in[l].astype(_bf16), w_pool[l].astype(_bf16), w_out[l].astype(_bf16)
        w_gate_b, w_up_b, w_down_b = w_gate[l].astype(_bf16), w_up[l].astype(_bf16), w_down[l].astype(_bf16)
        ffn = functools.partial(_merge_ffn, w_pool=w_pool_b, pool_scale=pool_scale[l], w_out=w_out_b,
                                ln2_g=ln2_g[l], w_gate=w_gate_b, w_up=w_up_b, w_down=w_down_b)

        qp, kp, vp, dp, gap, gbp, hist = _proj(x_p, ln1_g[l], w_in_b, q_norm_g[l], k_norm_g[l], windows=True)
        q, k, v, u, ga, gb = _proj(x_s.reshape(1, DB * DS, D), ln1_g[l], w_in_b, q_norm_g[l], k_norm_g[l],
                                   windows=False)
        per_seq = lambda a: a.reshape(N_KV_HEADS, DB, DS, HEAD_DIM).transpose(1, 0, 2, 3)
        k, v = per_seq(k), per_seq(v)

        attn_p, attn = _attention(qp, kp, vp, q.reshape(DB, DS, Q_COLS), k, v, cache_k, cache_v, page_table, l)

        flat = lambda a: a.reshape(B * S, a.shape[-1])
        x_p = ffn(flat(x_p), flat(attn_p), flat(dp), flat(gap), flat(gbp)).reshape(B, S, D)
        kp_l.append(kp)
        vp_l.append(vp)
        pp_l.append(hist[:, HIST_ROWS - POOL_HIST:])

        u_ext = jnp.concatenate(
            [jnp.zeros((DB, HIST_ROWS - POOL_HIST, POOL_WIDTH), _f32), state_pool[l], u.reshape(DB, DS, POOL_WIDTH)],
            axis=1)
        d = _pool_windows(u_ext, DS, past_len)
        flat = lambda a: a.reshape(DB * DS, a.shape[-1])
        x_s = ffn(flat(x_s), flat(attn), flat(d), flat(ga), flat(gb)).reshape(DB, DS, D)
        ks_l.append(k)
        vs_l.append(v)
        ps_l.append(u_ext[:, HIST_ROWS + DS - POOL_HIST:])
    return (x_p, x_s, jnp.stack(kp_l), jnp.stack(vp_l), jnp.stack(pp_l),
            jnp.stack(ks_l), jnp.stack(vs_l), jnp.stack(ps_l))
```

```python
import functools

import jax
import jax.numpy as jnp
from jax import lax
from jax.experimental import pallas as pl
from jax.experimental.pallas import tpu as pltpu

D_MODEL = 1024
N_HEADS = 8
HEAD_DIM = D_MODEL // N_HEADS
N_KV_HEADS = 4
KV_GROUP = N_HEADS // N_KV_HEADS
MOBA_BLOCK = 256
MOBA_TOPK = 3
PAGE_SIZE = 128
ATTN_SCALE = HEAD_DIM ** -0.5
POOL_WINDOWS = (2, 4, 8, 16)
N_POOL_GROUPS = len(POOL_WINDOWS)
POOL_WIDTH = D_MODEL
POOL_GROUP_DIM = POOL_WIDTH // N_POOL_GROUPS
POOL_HIST = max(POOL_WINDOWS) - 1
RMS_EPS = 1e-6
Q_COLS = N_HEADS * HEAD_DIM
KV_COLS = N_KV_HEADS * HEAD_DIM
COL_Q = 0
COL_K = Q_COLS
COL_V = COL_K + KV_COLS
COL_U = COL_V + KV_COLS
COL_GA = COL_U + POOL_WIDTH
COL_GB = COL_GA + D_MODEL
IN_COLS = COL_GB + D_MODEL

SUBLANES = 8
LANES = 128
MXU_DIM = 256
VMEM_LIMIT_BYTES = 56 * 1024 * 1024

HIST_ROWS = 16
assert HIST_ROWS >= POOL_HIST and HIST_ROWS % SUBLANES == 0
MASK_NEG = -(2.0 ** 30)
LOG2_E = 1.4426950408889634

TOKEN_TILE = 256
ROW_CHUNK = 32
SCORE_SLOTS = 3
PAGES_PER_STEP = 16
BLOCKS_PER_STEP = PAGES_PER_STEP * PAGE_SIZE // MOBA_BLOCK
PAGE_SLOTS = 4
PREFETCH_JOBS = PAGE_SLOTS - 1

_f32 = jnp.float32
_bf16 = jnp.bfloat16


def _nt_dot(a, b, precision=None):
    return lax.dot_general(a, b, (((1,), (1,)), ((), ())), precision=precision,
                           preferred_element_type=_f32)


def _dot(a, b):
    return jnp.dot(a, b, preferred_element_type=_f32)


def _rms(x, g):
    return x * lax.rsqrt(jnp.mean(x * x, axis=-1, keepdims=True) + RMS_EPS) * g


def _sigmoid(x):
    return 1.0 / (1.0 + jnp.exp(-x))


def _resident(shape):
    nd = len(shape)
    return pl.BlockSpec(shape, lambda *_: (0,) * nd, pipeline_mode=pl.Buffered(1))


def _proj_kernel(x_ref, ln1_ref, w_ref, qg_ref, kg_ref,
                 q_ref, k_ref, v_ref, u_ref, ga_ref, gb_ref, *rest, tm, windows):
    if windows:
        hist_ref, carry_ref = rest
        si = pl.program_id(1)

        @pl.when(si == 0)
        def _():
            carry_ref[...] = jnp.zeros((HIST_ROWS, POOL_WIDTH), _f32)

    xn = _rms(x_ref[0], ln1_ref[...]).astype(_bf16)

    zu = _dot(xn, w_ref[:, COL_U:COL_GA])
    ga_ref[0] = _sigmoid(_dot(xn, w_ref[:, COL_GA:COL_GB]))
    gb_ref[0] = _sigmoid(_dot(xn, w_ref[:, COL_GB:IN_COLS]))
    if windows:
        pos = si * tm + lax.broadcasted_iota(jnp.int32, (tm, 1), 0)
        for g, w in enumerate(POOL_WINDOWS):
            sl = slice(g * POOL_GROUP_DIM, (g + 1) * POOL_GROUP_DIM)
            run = jnp.concatenate([carry_ref[:, sl], zu[:, sl]], axis=0)
            half = 1
            while half < w:
                run = run + pltpu.roll(run, half, axis=0)
                half *= 2
            cnt = jnp.minimum(w, pos + 1).astype(_f32)
            u_ref[0, :, sl] = run[HIST_ROWS:] / cnt - zu[:, sl]
        hist_ref[0] = zu[tm - HIST_ROWS:tm, :]
        carry_ref[...] = zu[tm - HIST_ROWS:tm, :]
    else:
        u_ref[0] = zu

    zq = _dot(xn, w_ref[:, COL_Q:COL_K])
    qg = qg_ref[...]
    for h in range(N_HEADS):
        sl = slice(h * HEAD_DIM, (h + 1) * HEAD_DIM)
        q_ref[0, :, sl] = _rms(zq[:, sl], qg)

    zk = _dot(xn, w_ref[:, COL_K:COL_V])
    zv = _dot(xn, w_ref[:, COL_V:COL_U])
    kg = kg_ref[...]
    for h in range(N_KV_HEADS):
        sl = slice(h * HEAD_DIM, (h + 1) * HEAD_DIM)
        k_ref[0, h] = _rms(zk[:, sl], kg)
        v_ref[0, h] = zv[:, sl]


def _proj(x, ln1_g, w_in_bf16, q_norm_g, k_norm_g, *, windows):
    B, S, D = x.shape
    tm = min(TOKEN_TILE, S)
    assert S % tm == 0 and tm % HIST_ROWS == 0
    tok = lambda n: pl.BlockSpec((1, tm, n), lambda b, s: (b, s, 0))
    kv = pl.BlockSpec((1, N_KV_HEADS, tm, HEAD_DIM), lambda b, s: (b, 0, s, 0))
    out_shape = [
        jax.ShapeDtypeStruct((B, S, Q_COLS), _f32),
        jax.ShapeDtypeStruct((B, N_KV_HEADS, S, HEAD_DIM), _f32),
        jax.ShapeDtypeStruct((B, N_KV_HEADS, S, HEAD_DIM), _f32),
        jax.ShapeDtypeStruct((B, S, POOL_WIDTH), _f32),
        jax.ShapeDtypeStruct((B, S, D_MODEL), _f32),
        jax.ShapeDtypeStruct((B, S, D_MODEL), _f32),
    ]
    out_specs = [tok(Q_COLS), kv, kv, tok(POOL_WIDTH), tok(D_MODEL), tok(D_MODEL)]
    scratch = []
    if windows:
        out_shape.append(jax.ShapeDtypeStruct((B, HIST_ROWS, POOL_WIDTH), _f32))
        out_specs.append(pl.BlockSpec((1, HIST_ROWS, POOL_WIDTH), lambda b, s: (b, 0, 0)))
        scratch.append(pltpu.VMEM((HIST_ROWS, POOL_WIDTH), _f32))
    return pl.pallas_call(
        functools.partial(_proj_kernel, tm=tm, windows=windows),
        out_shape=out_shape,
        grid=(B, S // tm),
        in_specs=[tok(D), _resident((1, D)), _resident((D, IN_COLS)),
                  _resident((1, HEAD_DIM)), _resident((1, HEAD_DIM))],
        out_specs=out_specs,
        scratch_shapes=scratch,
        compiler_params=pltpu.CompilerParams(
            dimension_semantics=("arbitrary", "arbitrary"), vmem_limit_bytes=VMEM_LIMIT_BYTES),
        name="proj_windows" if windows else "proj",
    )(x, ln1_g.reshape(1, D), w_in_bf16, q_norm_g.reshape(1, HEAD_DIM), k_norm_g.reshape(1, HEAD_DIM))


def _topk_select(g, valid, row_idx, n_rows, axis):
    gm = jnp.where(valid, g, -jnp.inf)
    cnt = jnp.zeros(g.shape, jnp.int32)
    for jp in range(n_rows):
        gj = lax.slice_in_dim(gm, jp, jp + 1, axis=axis)
        beats = (gj > gm) | ((gj == gm) & (jp < row_idx))
        cnt = cnt + jnp.where(beats, 1, 0)
    return valid & (cnt < MOBA_TOPK)


def _split_bf16(x):
    hi = x.astype(_bf16)
    return hi, (x - hi.astype(_f32)).astype(_bf16)


N_PARTS = 3


def _split3(x):
    parts = []
    for _ in range(N_PARTS):
        part = x.astype(_bf16).astype(_f32)
        parts.append(part)
        x = x - part
    return parts


def _prompt_program(q_ref, k_ref, v_ref, slope_ref, o_ref, kaug_ref, vt_ref, s_ref, p_ref, *,
                    first_step, nb, nbp):
    S = nb * MOBA_BLOCK
    QR = KV_GROUP * MOBA_BLOCK
    KM_ROWS = 2 * SUBLANES

    k = k_ref[0, 0]
    kaug_ref[:, 0:HEAD_DIM] = k.astype(_bf16)
    @pl.when(first_step)
    def _():
        row = lax.broadcasted_iota(jnp.int32, (S, LANES), 0)
        lane = lax.broadcasted_iota(jnp.int32, (S, LANES), 1)
        onehot = jnp.where((lane % nbp == row // MOBA_BLOCK) & (lane < N_PARTS * nbp), 1.0, 0.0)
        is_rpos = (lane >= N_PARTS * nbp) & (lane < N_PARTS * (nbp + SUBLANES)) & (lane % SUBLANES == 0)
        rpos = jnp.where(is_rpos, (row % MOBA_BLOCK).astype(_f32), 0.0)
        kaug_ref[:, HEAD_DIM:HEAD_DIM + LANES] = (onehot + rpos).astype(_bf16)

    vt_ref[...] = v_ref[0, 0].T.astype(_bf16)

    kmean = jnp.mean(k.reshape(nb, MOBA_BLOCK, HEAD_DIM), axis=1)
    km = jnp.concatenate([kmean, jnp.zeros((KM_ROWS - nb, HEAD_DIM), _f32)], axis=0)
    km_hi, km_lo = _split_bf16(km)
    km_cat = jnp.concatenate([km_hi, km_hi, km_lo], axis=1)

    slope_rows = slope_ref[0]
    slope = slope_rows[0:1, :]
    jidx = lax.broadcasted_iota(jnp.int32, (nbp, QR), 0)
    key_r = lax.broadcasted_iota(jnp.int32, (MOBA_BLOCK, QR), 0)
    t_rel = lax.broadcasted_iota(jnp.int32, (MOBA_BLOCK, QR), 1) % MOBA_BLOCK
    causal = key_r <= t_rel

    def augmented_queries(i):
        rows = slice(i * MOBA_BLOCK, (i + 1) * MOBA_BLOCK)
        q2 = q_ref[0, rows, :]
        q_t = jnp.concatenate([q2[:, h * HEAD_DIM:(h + 1) * HEAD_DIM].T for h in range(KV_GROUP)], axis=1)

        if i > MOBA_TOPK:
            q_hi, q_lo = _split_bf16(q_t)
            gate_t = _dot(km_cat, jnp.concatenate([q_hi, q_lo, q_hi], axis=0))[0:nbp]
            sel = _topk_select(gate_t, jidx < i, jidx, i, axis=0)
        else:
            sel = jidx < i
        bias_t = jnp.where(sel, slope * (MOBA_BLOCK * (jidx - i)).astype(_f32), MASK_NEG)
        bias_t = jnp.where(jidx == i, 0.0, bias_t)
        aux_t = jnp.concatenate(
            _split3(bias_t * LOG2_E) + _split3(slope_rows * LOG2_E)
            + [jnp.zeros((LANES - N_PARTS * (nbp + SUBLANES), QR), _f32)], axis=0)
        return jnp.concatenate([(q_t * (ATTN_SCALE * LOG2_E)).astype(_bf16), aux_t.astype(_bf16)], axis=0)

    row_max = {}

    def scores(i):
        qa_t = augmented_queries(i)
        mt = None
        for j in range(i + 1):
            blk = slice(j * MOBA_BLOCK, (j + 1) * MOBA_BLOCK)
            sj = _dot(kaug_ref[blk, :], qa_t)
            if j == i:
                sj = jnp.where(causal, sj, MASK_NEG)
            s_ref[i % SCORE_SLOTS, blk, :] = sj
            for r in range(0, MOBA_BLOCK, ROW_CHUNK):
                mt = sj[r:r + ROW_CHUNK] if mt is None else jnp.maximum(mt, sj[r:r + ROW_CHUNK])
        row_max[i] = jnp.max(mt, axis=0, keepdims=True)

    def finish(i):
        m = row_max.pop(i)
        n = (i + 1) * MOBA_BLOCK
        lt = jnp.zeros((ROW_CHUNK, QR), _f32)
        for r in range(0, n, ROW_CHUNK):
            p = jnp.exp2(s_ref[i % SCORE_SLOTS, r:r + ROW_CHUNK, :] - m)
            lt = lt + p
            p_ref[i % 2, r:r + ROW_CHUNK, :] = p.astype(_bf16)
        l = jnp.sum(lt, axis=0, keepdims=True)
        o_t = _dot(vt_ref[:, 0:n], p_ref[i % 2, 0:n, :]) / l
        o_ref[0, i * MOBA_BLOCK:n, :] = jnp.concatenate(
            [o_t[:, h * MOBA_BLOCK:(h + 1) * MOBA_BLOCK].T for h in range(KV_GROUP)], axis=1)

    ahead = SCORE_SLOTS - 1
    for i in range(min(ahead, nb)):
        scores(i)

    def block(i):
        if i + ahead < nb:
            scores(i + ahead)
        finish(i)

    return block


def _slope_table(rows_per_head):
    slopes = 2.0 ** (-8.0 * jnp.arange(1, N_HEADS + 1, dtype=_f32) / N_HEADS)
    per_row = jnp.repeat(slopes.reshape(N_KV_HEADS, KV_GROUP), rows_per_head, axis=1)
    tab = jnp.zeros((N_KV_HEADS, SUBLANES, KV_GROUP * rows_per_head), _f32)
    return tab.at[:, 0, :].set(per_row)


def _sample_program(pt_ref, qbd_hi_ref, qbd_lo_ref, kn_ref, vn_ref, slope_row_ref, rtab_ref, ck_hbm, cv_hbm,
                    o_ref, pbuf, sem, s_ref, p_ref, ksum_ref, bmax_ref, *, b, layer, nc, ds, nblk, n_seq):
    P = PAGES_PER_STEP
    njobs = 2 * nc
    QR = KV_GROUP * ds
    R = N_KV_HEADS * QR

    def chunk_copies(seq, job):
        src = ck_hbm if job < nc else cv_hbm
        slot = job % PAGE_SLOTS
        return [pltpu.make_async_copy(src.at[layer, pt_ref[seq, (job % nc) * P + pg]], pbuf.at[slot, pg], sem.at[slot])
                for pg in range(P)]

    def start(seq, job):
        for cp in chunk_copies(seq, job):
            cp.start()

    @pl.when(b == 0)
    def _():
        for job in range(PREFETCH_JOBS):
            start(b, job)

    def begin_job(job):
        for cp in chunk_copies(b, job):
            cp.wait()
        ahead = job + PREFETCH_JOBS
        if ahead < njobs:
            start(b, ahead)
        else:
            @pl.when(b + 1 < n_seq)
            def _():
                start(b + 1, ahead - njobs)
        return job % PAGE_SLOTS

    CW = P * PAGE_SIZE
    state = {}

    def to_rows(x_t):
        if x_t.shape[0] < LANES:
            x_t = jnp.concatenate([x_t, jnp.zeros((LANES - x_t.shape[0], LANES), _f32)], axis=0)
        return x_t.T[0:R]

    def key_job(c):
        slot = begin_job(c)
        rows, sums = [], [[] for _ in range(N_KV_HEADS)]
        for pg in range(P):
            heads = []
            for h in range(N_KV_HEADS):
                kp = pbuf[slot, pg, h]
                sums[h].append(jnp.sum(kp, axis=0, keepdims=True))
                heads.append(kp.astype(_bf16))
            rows.append(jnp.concatenate(heads, axis=1))
        s = _dot(jnp.concatenate(rows, axis=0), qbd_hi_ref[0])
        rtab = rtab_ref[...]
        for jb in range(BLOCKS_PER_STEP):
            j = c * BLOCKS_PER_STEP + jb
            sb = s[jb * MOBA_BLOCK:(jb + 1) * MOBA_BLOCK] + rtab
            s_ref[j * MOBA_BLOCK:(j + 1) * MOBA_BLOCK, :] = sb
            bmax_ref[j:j + 1, :] = jnp.max(sb, axis=0, keepdims=True)
        ppb = MOBA_BLOCK // PAGE_SIZE
        for h in range(N_KV_HEADS):
            bs = [functools.reduce(lambda a, b: a + b, sums[h][j * ppb:(j + 1) * ppb]) for j in range(P // ppb)]
            ksum_ref[c * BLOCKS_PER_STEP:(c + 1) * BLOCKS_PER_STEP, h * HEAD_DIM:(h + 1) * HEAD_DIM] = (
                jnp.concatenate(bs, axis=0))

    def select():
        qbd_hi, qbd_lo = qbd_hi_ref[0], qbd_lo_ref[0]
        km_hi, km_lo = _split_bf16(ksum_ref[...] / MOBA_BLOCK)
        gate_t = _dot(km_hi, qbd_hi) + _dot(km_lo, qbd_hi) + _dot(km_hi, qbd_lo)
        jidx = lax.broadcasted_iota(jnp.int32, (nblk, LANES), 0)
        sel = _topk_select(gate_t, jidx < nblk, jidx, nblk, axis=0)
        slope = slope_row_ref[0:1, :]
        bias_t = jnp.where(sel, slope * (MOBA_BLOCK * (jidx - nblk)).astype(_f32), MASK_NEG)
        mt = bmax_ref[...] + bias_t
        xn = jnp.concatenate([kn_ref[0, h] for h in range(N_KV_HEADS)], axis=1)
        xn = jnp.concatenate([xn, jnp.zeros((2 * SUBLANES - ds, xn.shape[1]), _f32)], axis=0)
        so = _dot(xn.astype(_bf16), qbd_hi)[0:ds]
        ss = lax.broadcasted_iota(jnp.int32, (ds, LANES), 0)
        tt = lax.broadcasted_iota(jnp.int32, (ds, LANES), 1) % ds
        so = jnp.where(ss <= tt, so + slope * ss.astype(_f32), MASK_NEG)
        m = jnp.maximum(jnp.max(mt, axis=0, keepdims=True), jnp.max(so, axis=0, keepdims=True))
        po = jnp.exp2(so - m)
        l = jnp.sum(po, axis=0, keepdims=True)
        shift = bias_t - m
        lt = jnp.zeros((PAGE_SIZE, LANES), _f32)
        for g in range(nblk * MOBA_BLOCK // PAGE_SIZE):
            j = g * PAGE_SIZE // MOBA_BLOCK
            p = jnp.exp2(s_ref[g * PAGE_SIZE:(g + 1) * PAGE_SIZE, :] + shift[j:j + 1, :])
            lt = lt + p
            col = (g * PAGE_SIZE) % CW
            p_ref[g * PAGE_SIZE // CW, :, col:col + PAGE_SIZE] = to_rows(p).astype(_bf16)
        l = l + jnp.sum(lt, axis=0, keepdims=True)
        state["l"] = to_rows(jnp.broadcast_to(l, (SUBLANES, LANES)))[:, 0:1]
        po_rows = to_rows(po)[:, 0:ds].astype(_bf16)
        state["acc"] = [_dot(po_rows[h * QR:(h + 1) * QR], vn_ref[0, h].astype(_bf16))
                        for h in range(N_KV_HEADS)]

    def value_job(c):
        slot = begin_job(nc + c)
        acc = state["acc"]
        p = p_ref[c]
        for h in range(N_KV_HEADS):
            vh = jnp.concatenate([pbuf[slot, pg, h].astype(_bf16) for pg in range(P)], axis=0)
            acc[h] = acc[h] + _dot(p[h * QR:(h + 1) * QR], vh)

    def write_out():
        for h in range(N_KV_HEADS):
            o = state["acc"][h] / state["l"][h * QR:(h + 1) * QR]
            for g in range(KV_GROUP):
                hh = KV_GROUP * h + g
                o_ref[0, :, hh * HEAD_DIM:(hh + 1) * HEAD_DIM] = o[g * ds:(g + 1) * ds]

    def job(j):
        if j < nc:
            key_job(j)
            if j == nc - 1:
                select()
        else:
            value_job(j - nc)
            if j == njobs - 1:
                write_out()

    return job


def _attn_kernel(pt_ref, qp_ref, kp_ref, vp_ref, pslope_ref,
                 qbd_hi_ref, qbd_lo_ref, kn_ref, vn_ref, slope_row_ref, rtab_ref, ck_hbm, cv_hbm,
                 op_ref, os_ref,
                 kaug_ref, vt_ref, ps_ref, pp_ref, pbuf, sem, ss_ref, sp_ref, ksum_ref, bmax_ref,
                 *, nb, nbp, layer, nc, ds, nblk, n_seq):
    sample_job = _sample_program(pt_ref, qbd_hi_ref, qbd_lo_ref, kn_ref, vn_ref, slope_row_ref, rtab_ref,
                                 ck_hbm, cv_hbm, os_ref, pbuf, sem, ss_ref, sp_ref, ksum_ref, bmax_ref,
                                 b=pl.program_id(0), layer=layer, nc=nc, ds=ds, nblk=nblk, n_seq=n_seq)
    prompt_block = _prompt_program(qp_ref, kp_ref, vp_ref, pslope_ref, op_ref, kaug_ref, vt_ref, ps_ref, pp_ref,
                                   first_step=pl.program_id(0) == 0, nb=nb, nbp=nbp)
    jobs_per_block = 2 * nc // nb
    for i in range(nb):
        for j in range(i * jobs_per_block, (i + 1) * jobs_per_block):
            sample_job(j)
        prompt_block(i)


def _attention(qp, kp, vp, q, k_new, v_new, cache_k, cache_v, page_table, layer):
    B, S, _ = qp.shape
    assert S % MOBA_BLOCK == 0
    nb = S // MOBA_BLOCK
    nbp = -(-nb // SUBLANES) * SUBLANES
    assert N_PARTS * (nbp + SUBLANES) <= LANES and nb <= 2 * SUBLANES
    gw = KV_GROUP * HEAD_DIM
    PQR = KV_GROUP * MOBA_BLOCK

    DB, DS, _ = q.shape
    assert DB == B * N_KV_HEADS
    n_pages = page_table.shape[1]
    past = n_pages * PAGE_SIZE
    assert past % MOBA_BLOCK == 0 and DS <= MOBA_BLOCK and DS % SUBLANES == 0
    nblk = past // MOBA_BLOCK
    assert nblk >= MOBA_TOPK and nblk % SUBLANES == 0 and nblk + SUBLANES <= LANES
    P = PAGES_PER_STEP
    assert n_pages % P == 0 and P % (MOBA_BLOCK // PAGE_SIZE) == 0
    nc = n_pages // P
    QR = KV_GROUP * DS
    R = N_KV_HEADS * QR
    assert R <= LANES and QR % (2 * SUBLANES) == 0
    CW = P * PAGE_SIZE

    slopes = LOG2_E * 2.0 ** (-8.0 * jnp.arange(1, N_HEADS + 1, dtype=_f32) / N_HEADS)
    per_lane = jnp.zeros((LANES,), _f32).at[:R].set(jnp.repeat(slopes, DS))
    slope_row = jnp.zeros((SUBLANES, LANES), _f32).at[0].set(per_lane)
    rtab = jnp.arange(MOBA_BLOCK, dtype=_f32)[:, None] * per_lane[None, :]

    def embed(part):
        qh = part.reshape(DB, DS, N_KV_HEADS, KV_GROUP, HEAD_DIM).transpose(0, 2, 4, 3, 1)
        qh = qh.reshape(DB, N_KV_HEADS, HEAD_DIM, 1, QR)
        own_head = jnp.eye(N_KV_HEADS, dtype=jnp.bool_)[None, :, None, :, None]
        qbd = jnp.where(own_head, qh, jnp.zeros((), _bf16)).reshape(DB, KV_COLS, R)
        return jnp.pad(qbd, ((0, 0), (0, 0), (0, LANES - R)))

    q_scaled = q * (ATTN_SCALE * LOG2_E)
    q_hi = q_scaled.astype(_bf16)
    qbd_hi = embed(q_hi)
    qbd_lo = embed((q_scaled - q_hi.astype(_f32)).astype(_bf16))

    assert (2 * nc) % PAGE_SLOTS == 0 and PREFETCH_JOBS == PAGE_SLOTS - 1 and PREFETCH_JOBS <= 2 * nc

    assert (2 * nc) % nb == 0

    qo_spec = pl.BlockSpec((1, S, gw), lambda g, pt: (g // N_KV_HEADS, 0, g % N_KV_HEADS))
    kv_spec = pl.BlockSpec((1, 1, S, HEAD_DIM), lambda g, pt: (g // N_KV_HEADS, g % N_KV_HEADS, 0, 0))
    tok_spec = pl.BlockSpec((1, DS, Q_COLS), lambda g, pt: (g, 0, 0))
    new_spec = pl.BlockSpec((1, N_KV_HEADS, DS, HEAD_DIM), lambda g, pt: (g, 0, 0, 0))
    qbd_spec = pl.BlockSpec((1, KV_COLS, LANES), lambda g, pt: (g, 0, 0))
    hbm = pl.BlockSpec(memory_space=pl.ANY)
    in_specs = [qo_spec, kv_spec, kv_spec,
                pl.BlockSpec((1, SUBLANES, PQR), lambda g, pt: (g % N_KV_HEADS, 0, 0)),
                qbd_spec, qbd_spec, new_spec, new_spec,
                pl.BlockSpec((SUBLANES, LANES), lambda g, pt: (0, 0)),
                pl.BlockSpec((MOBA_BLOCK, LANES), lambda g, pt: (0, 0)),
                hbm, hbm]

    return pl.pallas_call(
        functools.partial(_attn_kernel, nb=nb, nbp=nbp, layer=layer, nc=nc, ds=DS, nblk=nblk, n_seq=DB),
        out_shape=[jax.ShapeDtypeStruct((B, S, Q_COLS), _f32), jax.ShapeDtypeStruct((DB, DS, Q_COLS), _f32)],
        grid_spec=pltpu.PrefetchScalarGridSpec(
            num_scalar_prefetch=1,
            grid=(DB,),
            in_specs=in_specs,
            out_specs=[qo_spec, tok_spec],
            scratch_shapes=[pltpu.VMEM((S, HEAD_DIM + LANES), _bf16),
                            pltpu.VMEM((HEAD_DIM, S), _bf16),
                            pltpu.VMEM((SCORE_SLOTS, S, PQR), _f32),
                            pltpu.VMEM((2, S, PQR), _bf16),
                            pltpu.VMEM((PAGE_SLOTS, P, N_KV_HEADS, PAGE_SIZE, HEAD_DIM), _f32),
                            pltpu.SemaphoreType.DMA((PAGE_SLOTS,)),
                            pltpu.VMEM((past, LANES), _f32),
                            pltpu.VMEM((nc, R, CW), _bf16),
                            pltpu.VMEM((nblk, KV_COLS), _f32),
                            pltpu.VMEM((nblk, LANES), _f32)]),
        compiler_params=pltpu.CompilerParams(
            dimension_semantics=("arbitrary",), vmem_limit_bytes=VMEM_LIMIT_BYTES),
        name="attention",
    )(page_table, qp, kp, vp, _slope_table(MOBA_BLOCK), qbd_hi, qbd_lo, k_new, v_new, slope_row, rtab,
      cache_k, cache_v)


def _pool_windows_kernel(uext_ref, d_ref, *, ds, first_pos):
    cur = uext_ref[:, HIST_ROWS:HIST_ROWS + ds, :]
    pos = first_pos + lax.broadcasted_iota(jnp.int32, (1, ds, 1), 1)
    for g, w in enumerate(POOL_WINDOWS):
        sl = slice(g * POOL_GROUP_DIM, (g + 1) * POOL_GROUP_DIM)
        win = cur[:, :, sl]
        for dd in range(1, w):
            win = win + uext_ref[:, HIST_ROWS - dd:HIST_ROWS - dd + ds, sl]
        cnt = jnp.minimum(w, pos + 1).astype(_f32)
        d_ref[:, :, sl] = win / cnt - cur[:, :, sl]


def _pool_windows(u_ext_padded, ds, first_pos):
    DB, R, P = u_ext_padded.shape
    assert R == HIST_ROWS + ds
    bb = 16
    assert DB % bb == 0
    return pl.pallas_call(
        functools.partial(_pool_windows_kernel, ds=ds, first_pos=first_pos),
        out_shape=jax.ShapeDtypeStruct((DB, ds, P), _f32),
        grid=(DB // bb,),
        in_specs=[pl.BlockSpec((bb, R, P), lambda b: (b, 0, 0))],
        out_specs=pl.BlockSpec((bb, ds, P), lambda b: (b, 0, 0)),
        compiler_params=pltpu.CompilerParams(dimension_semantics=("arbitrary",)),
        name="pool_windows",
    )(u_ext_padded)


def _merge_ffn_kernel(x_ref, attn_ref, d_ref, sga_ref, sgb_ref, wpool_ref, pscale_ref, wout_ref, ln2_ref,
                      wgate_ref, wup_ref, wdown_ref, y_ref):
    d = d_ref[...]
    pool = jnp.concatenate(
        [_dot(d[:, g * POOL_GROUP_DIM:(g + 1) * POOL_GROUP_DIM].astype(_bf16), wpool_ref[g])
         for g in range(N_POOL_GROUPS)], axis=1) * pscale_ref[...]
    h = (sga_ref[...] * attn_ref[...] + sgb_ref[...] * pool).astype(_bf16)
    tm = h.shape[0]
    halves = [slice(0, tm // 2), slice(tm // 2, tm)]
    x1 = [x_ref[r, :] + _dot(h[r], wout_ref[...]) for r in halves]
    hn = [_rms(x1[a], ln2_ref[...]).astype(_bf16) for a in range(2)]
    gate = [None, None]
    up = [None, None]
    act = [None, None]
    gate[0] = _dot(hn[0], wgate_ref[...])
    up[0] = _dot(hn[0], wup_ref[...])
    gate[1] = _dot(hn[1], wgate_ref[...])
    act[0] = (gate[0] * _sigmoid(gate[0]) * up[0]).astype(_bf16)
    up[1] = _dot(hn[1], wup_ref[...])
    y_ref[halves[0], :] = x1[0] + _dot(act[0], wdown_ref[...])
    act[1] = (gate[1] * _sigmoid(gate[1]) * up[1]).astype(_bf16)
    y_ref[halves[1], :] = x1[1] + _dot(act[1], wdown_ref[...])


def _merge_ffn(x, attn, d, ga, gb, w_pool, pool_scale, w_out, ln2_g, w_gate, w_up, w_down):
    N, D = x.shape
    tm = min(2 * TOKEN_TILE, N)
    assert N % tm == 0 and tm % SUBLANES == 0
    d_ff = w_gate.shape[1]
    tok = pl.BlockSpec((tm, D), lambda t: (t, 0))
    return pl.pallas_call(
        _merge_ffn_kernel,
        out_shape=jax.ShapeDtypeStruct((N, D), _f32),
        grid=(N // tm,),
        in_specs=[tok, tok, tok, tok, tok,
                  _resident((N_POOL_GROUPS, POOL_GROUP_DIM, POOL_GROUP_DIM)), _resident((1, POOL_WIDTH)),
                  _resident((D, D)), _resident((1, D)),
                  _resident((D, d_ff)), _resident((D, d_ff)), _resident((d_ff, D))],
        out_specs=tok,
        compiler_params=pltpu.CompilerParams(
            dimension_semantics=("arbitrary",), vmem_limit_bytes=VMEM_LIMIT_BYTES),
        name="merge_ffn",
    )(x, attn, d, ga, gb, w_pool, pool_scale.reshape(1, POOL_WIDTH), w_out, ln2_g.reshape(1, D),
      w_gate, w_up, w_down)


def kernel(x_prompt, x_sample, cache_k, cache_v, state_pool, page_table, ln1_g, w_in, q_norm_g, k_norm_g,
           w_pool, pool_scale, w_out, ln2_g, w_gate, w_up, w_down):
    depth = w_in.shape[0]
    B, S, D = x_prompt.shape
    DB, DS, _ = x_sample.shape
    past_len = page_table.shape[1] * PAGE_SIZE
    x_p, x_s = x_prompt, x_sample
    kp_l, vp_l, pp_l, ks_l, vs_l, ps_l = [], [], [], [], [], []
    for l in range(depth):
        w_in_b, w_pool_b, w_out_b = w_in[l].astype(_bf16), w_pool[l].astype(_bf16), w_out[l].astype(_bf16)
        w_gate_b, w_up_b, w_down_b = w_gate[l].astype(_bf16), w_up[l].astype(_bf16), w_down[l].astype(_bf16)
        ffn = functools.partial(_merge_ffn, w_pool=w_pool_b, pool_scale=pool_scale[l], w_out=w_out_b,
                                ln2_g=ln2_g[l], w_gate=w_gate_b, w_up=w_up_b, w_down=w_down_b)

        qp, kp, vp, dp, gap, gbp, hist = _proj(x_p, ln1_g[l], w_in_b, q_norm_g[l], k_norm_g[l], windows=True)
        q, k, v, u, ga, gb = _proj(x_s.reshape(1, DB * DS, D), ln1_g[l], w_in_b, q_norm_g[l], k_norm_g[l],
                                   windows=False)
        per_seq = lambda a: a.reshape(N_KV_HEADS, DB, DS, HEAD_DIM).transpose(1, 0, 2, 3)
        k, v = per_seq(k), per_seq(v)

        attn_p, attn = _attention(qp, kp, vp, q.reshape(DB, DS, Q_COLS), k, v, cache_k, cache_v, page_table, l)

        flat = lambda a: a.reshape(B * S, a.shape[-1])
        x_p = ffn(flat(x_p), flat(attn_p), flat(dp), flat(gap), flat(gbp)).reshape(B, S, D)
        kp_l.append(kp)
        vp_l.append(vp)
        pp_l.append(hist[:, HIST_ROWS - POOL_HIST:])

        u_ext = jnp.concatenate(
            [jnp.zeros((DB, HIST_ROWS - POOL_HIST, POOL_WIDTH), _f32), state_pool[l], u.reshape(DB, DS, POOL_WIDTH)],
            axis=1)
        d = _pool_windows(u_ext, DS, past_len)
        flat = lambda a: a.reshape(DB * DS, a.shape[-1])
        x_s = ffn(flat(x_s), flat(attn), flat(d), flat(ga), flat(gb)).reshape(DB, DS, D)
        ks_l.append(k)
        vs_l.append(v)
        ps_l.append(u_ext[:, HIST_ROWS + DS - POOL_HIST:])
    return (x_p, x_s, jnp.stack(kp_l), jnp.stack(vp_l), jnp.stack(pp_l),
            jnp.stack(ks_l), jnp.stack(vs_l), jnp.stack(ps_l))
```
